```python
import math
import jax, jax.numpy as jnp
from jax import lax
import numpy as np

D_MODEL = 2048
BATCH = 8
SEQ = 2048
DEPTH = 4
DEC_BATCH = 2
DEC_SEQ = 4096
PAST_LEN = 128

N_MIXERS = 2
N_DIFF = (DEPTH + 1) // 2
N_WIN = DEPTH // 2
DIFF_HEAD_DIM = 64
DIFF_HEADS = D_MODEL // (2 * DIFF_HEAD_DIM)
DIFF_IN = 6 * DIFF_HEADS * DIFF_HEAD_DIM
WIN_HEAD_DIM = 128
WIN_Q_HEADS = D_MODEL // WIN_HEAD_DIM
WIN_KV_HEADS = 4
WIN_IN = (WIN_Q_HEADS + 2 * WIN_KV_HEADS) * WIN_HEAD_DIM
WINDOW = 128
Q_BLOCK = 128
FFN_DIM = ((8 * D_MODEL // 3 + 127) // 128) * 128
CONV_WIDTH = 3
ROPE_THETA = 500000.0
ROPE_FRACTION = 4
ALPHA = (2 * DEPTH) ** 0.25
BETA = (8 * DEPTH) ** -0.25
LN_EPS = 1e-5

kernel_name = "hybrid_diffattn_swa_sink_convffn_encoder"


def layer_norm(x, g, b):
    xf = x.astype(jnp.float32)
    mu = jnp.mean(xf, axis=-1, keepdims=True)
    var = jnp.mean(jnp.square(xf - mu), axis=-1, keepdims=True)
    y = (xf - mu) * lax.rsqrt(var + LN_EPS) * g.astype(jnp.float32) + b.astype(jnp.float32)
    return y.astype(x.dtype)


def rope_tables(seq, rot_dim):
    pos = jnp.arange(seq, dtype=jnp.float32)
    inv_freq = ROPE_THETA ** (-jnp.arange(0, rot_dim, 2, dtype=jnp.float32) / rot_dim)
    ang = pos[:, None] * inv_freq[None, :]
    return jnp.cos(ang), jnp.sin(ang)


def apply_partial_rope(x, cos, sin):
    half = cos.shape[-1]
    rot = 2 * half
    c = cos[:, None, :].astype(x.dtype)
    s = sin[:, None, :].astype(x.dtype)
    x1 = x[..., :half]
    x2 = x[..., half:rot]
    return jnp.concatenate([x1 * c - x2 * s, x2 * c + x1 * s, x[..., rot:]], axis=-1)


def diff_attention(x, w_in, lam, subln_g, lambda_init):
    B, S, _ = x.shape
    H, d = DIFF_HEADS, DIFF_HEAD_DIM
    qkv = x @ w_in
    q, k, v = jnp.split(qkv, [2 * H * d, 4 * H * d], axis=-1)
    cos, sin = rope_tables(S, d // ROPE_FRACTION)
    q = apply_partial_rope(q.reshape(B, S, 2 * H, d), cos, sin).reshape(B, S, H, 2, d)
    k = apply_partial_rope(k.reshape(B, S, 2 * H, d), cos, sin).reshape(B, S, H, 2, d)
    v = v.reshape(B, S, H, 2 * d)
    lf = lam.astype(jnp.float32)
    lam_full = jnp.exp(jnp.sum(lf[0] * lf[1])) - jnp.exp(jnp.sum(lf[2] * lf[3])) + lambda_init
    g = subln_g.astype(jnp.float32)
    nb = S // Q_BLOCK
    qb = q.reshape(B, nb, Q_BLOCK, H, 2, d).transpose(1, 0, 2, 3, 4, 5)
    scale = d ** -0.5

    def block(qi):
        s = jnp.einsum('bqhcd,bkhcd->bhcqk', qi, k, preferred_element_type=jnp.float32) * scale
        p = jax.nn.softmax(s, axis=-1)
        a = p[:, :, 0] - lam_full * p[:, :, 1]
        o = jnp.einsum('bhqk,bkhe->bqhe', a.astype(v.dtype), v,
                       preferred_element_type=jnp.float32)
        o = o * lax.rsqrt(jnp.mean(jnp.square(o), axis=-1, keepdims=True) + LN_EPS) * g
        o = o * (1.0 - lambda_init)
        return o.reshape(B, Q_BLOCK, H * 2 * d).astype(x.dtype)

    o = lax.map(block, qb)
    return o.transpose(1, 0, 2, 3).reshape(B, S, H * 2 * d)


def window_attention(x, w_in, sink):
    B, S, _ = x.shape
    G, R, hd, W = WIN_KV_HEADS, WIN_Q_HEADS // WIN_KV_HEADS, WIN_HEAD_DIM, WINDOW
    nb = S // W
    qkv = x @ w_in
    q, k, v = jnp.split(qkv, [WIN_Q_HEADS * hd, (WIN_Q_HEADS + G) * hd], axis=-1)
    cos, sin = rope_tables(S, hd // ROPE_FRACTION)
    q = apply_partial_rope(q.reshape(B, S, WIN_Q_HEADS, hd), cos, sin).reshape(B, nb, W, G, R, hd)
    k = apply_partial_rope(k.reshape(B, S, G, hd), cos, sin)
    v = v.reshape(B, S, G, hd)

    def band(t):
        tp = jnp.pad(t, ((0, 0), (W, W), (0, 0), (0, 0))).reshape(B, nb + 2, W, G, hd)
        return jnp.concatenate([tp[:, :-2], tp[:, 1:-1], tp[:, 2:]], axis=2)

    kb, vb = band(k), band(v)
    s = jnp.einsum('bnqgrd,bnkgd->bngrqk', q, kb, preferred_element_type=jnp.float32) * hd ** -0.5
    qpos = jnp.arange(nb)[:, None, None] * W + jnp.arange(W)[None, :, None]
    kpos = jnp.arange(nb)[:, None, None] * W - W + jnp.arange(3 * W)[None, None, :]
    valid = (jnp.abs(qpos - kpos) <= W) & (kpos >= 0) & (kpos < S)
    s = jnp.where(valid[None, :, None, None], s, -jnp.inf)
    sk = sink.astype(jnp.float32).reshape(G, R)[None, None, :, :, None, None]
    m = jnp.maximum(jnp.max(s, axis=-1, keepdims=True), sk)
    e = jnp.exp(s - m)
    w = e / (jnp.sum(e, axis=-1, keepdims=True) + jnp.exp(sk - m))
    o = jnp.einsum('bngrqk,bnkgd->bnqgrd', w.astype(vb.dtype), vb)
    return o.reshape(B, S, WIN_Q_HEADS * hd)


def conv_ffn(x, w_up, conv_w, conv_b, w_down):
    h = x @ w_up
    C = h.shape[-1]
    h = lax.conv_general_dilated(h, conv_w[:, None, :], window_strides=(1,),
                                 padding=((CONV_WIDTH // 2, CONV_WIDTH // 2),),
                                 dimension_numbers=('NWC', 'WIO', 'NWC'),
                                 feature_group_count=C) + conv_b
    g, u = jnp.split(h, 2, axis=-1)
    return (jax.nn.silu(g) * u) @ w_down


def trunk(x, diff_w_in, diff_lam, diff_subln_g, diff_w_out, win_w_in, win_sink, win_w_out,
          ln_mix_g, ln_mix_b, ffn_w_up, ffn_conv_w, ffn_conv_b, ffn_w_down, ln_ffn_g, ln_ffn_b):
    for i in range(DEPTH):
        j = i // N_MIXERS
        if i % N_MIXERS == 0:
            lambda_init = 0.8 - 0.6 * math.exp(-0.3 * i)
            o = diff_attention(x, diff_w_in[j], diff_lam[j], diff_subln_g[j], lambda_init) @ diff_w_out[j]
        else:
            o = window_attention(x, win_w_in[j], win_sink[j]) @ win_w_out[j]
        x = layer_norm(ALPHA * x + o, ln_mix_g[i], ln_mix_b[i])
        f = conv_ffn(x, ffn_w_up[i], ffn_conv_w[i], ffn_conv_b[i], ffn_w_down[i])
        x = layer_norm(ALPHA * x + f, ln_ffn_g[i], ln_ffn_b[i])
    return x


def setup_inputs(seed: int = 0) -> dict:
    key = jax.random.key(seed)
    ks = jax.random.split(key, 20)
    f32 = jnp.float32
    nrm = lambda k, shape: jax.random.normal(k, shape, dtype=f32)
    d_in = D_MODEL ** -0.5
    qk_cols = 4 * DIFF_HEADS * DIFF_HEAD_DIM
    v_cols = DIFF_IN - qk_cols
    diff_w_in = jnp.concatenate([nrm(ks[2], (N_DIFF, D_MODEL, qk_cols)) * d_in,
                                 nrm(ks[3], (N_DIFF, D_MODEL, v_cols)) * d_in * BETA], axis=-1)
    win_qk_cols = (WIN_Q_HEADS + WIN_KV_HEADS) * WIN_HEAD_DIM
    win_w_in = jnp.concatenate([nrm(ks[4], (N_WIN, D_MODEL, win_qk_cols)) * d_in,
                                nrm(ks[5], (N_WIN, D_MODEL, WIN_IN - win_qk_cols)) * d_in * BETA], axis=-1)
    return {
        "x_prompt": nrm(ks[0], (BATCH, SEQ, D_MODEL)),
        "x_sample": nrm(ks[1], (DEC_BATCH, DEC_SEQ, D_MODEL)),
        "diff_w_in": diff_w_in,
        "diff_lam": nrm(ks[6], (N_DIFF, 4, DIFF_HEAD_DIM)) * 0.1,
        "diff_subln_g": 1.0 + 0.02 * nrm(ks[7], (N_DIFF, 2 * DIFF_HEAD_DIM)),
        "diff_w_out": nrm(ks[8], (N_DIFF, D_MODEL, D_MODEL)) * d_in * BETA,
        "win_w_in": win_w_in,
        "win_sink": nrm(ks[9], (N_WIN, WIN_Q_HEADS)) * 0.5,
        "win_w_out": nrm(ks[10], (N_WIN, D_MODEL, D_MODEL)) * d_in * BETA,
        "ln_mix_g": 1.0 + 0.02 * nrm(ks[11], (DEPTH, D_MODEL)),
        "ln_mix_b": 0.02 * nrm(ks[12], (DEPTH, D_MODEL)),
        "ffn_w_up": nrm(ks[13], (DEPTH, D_MODEL, 2 * FFN_DIM)) * d_in * BETA,
        "ffn_conv_w": nrm(ks[14], (DEPTH, CONV_WIDTH, 2 * FFN_DIM)) * CONV_WIDTH ** -0.5,
        "ffn_conv_b": 0.02 * nrm(ks[15], (DEPTH, 2 * FFN_DIM)),
        "ffn_w_down": nrm(ks[16], (DEPTH, FFN_DIM, D_MODEL)) * FFN_DIM ** -0.5 * BETA,
        "ln_ffn_g": 1.0 + 0.02 * nrm(ks[17], (DEPTH, D_MODEL)),
        "ln_ffn_b": 0.02 * nrm(ks[18], (DEPTH, D_MODEL)),
    }


def reference(x_prompt, x_sample, diff_w_in, diff_lam, diff_subln_g, diff_w_out, win_w_in, win_sink,
              win_w_out, ln_mix_g, ln_mix_b, ffn_w_up, ffn_conv_w, ffn_conv_b, ffn_w_down, ln_ffn_g, ln_ffn_b):
    y_prompt = trunk(x_prompt, diff_w_in, diff_lam, diff_subln_g, diff_w_out, win_w_in, win_sink, win_w_out,
                     ln_mix_g, ln_mix_b, ffn_w_up, ffn_conv_w, ffn_conv_b, ffn_w_down, ln_ffn_g, ln_ffn_b)
    y_sample = trunk(x_sample, diff_w_in, diff_lam, diff_subln_g, diff_w_out, win_w_in, win_sink, win_w_out,
                     ln_mix_g, ln_mix_b, ffn_w_up, ffn_conv_w, ffn_conv_b, ffn_w_down, ln_ffn_g, ln_ffn_b)
    return (y_prompt, y_sample)
```

```python
import functools
import math

import jax
import jax.numpy as jnp
from jax import lax
from jax.experimental import pallas as pl
from jax.experimental.pallas import tpu as pltpu

D_MODEL = 2048
DEPTH = 4
N_MIXERS = 2
DIFF_HEAD_DIM = 64
DIFF_HEADS = D_MODEL // (2 * DIFF_HEAD_DIM)
DIFF_IN = 6 * DIFF_HEADS * DIFF_HEAD_DIM
WIN_HEAD_DIM = 128
WIN_Q_HEADS = D_MODEL // WIN_HEAD_DIM
WIN_KV_HEADS = 4
WIN_REP = WIN_Q_HEADS // WIN_KV_HEADS
WIN_IN = (WIN_Q_HEADS + 2 * WIN_KV_HEADS) * WIN_HEAD_DIM
WINDOW = 128
FFN_DIM = ((8 * D_MODEL // 3 + 127) // 128) * 128
ROPE_THETA = 500000.0
ROPE_FRACTION = 4
ALPHA = (2 * DEPTH) ** 0.25
LN_EPS = 1e-5

LANES = 128
BF16_ROWS = 16
VMEM_LIMIT = 56 * 1024 * 1024

QKV_TM = 1024
QKV_TN = 512
PROJ_TM = 512
FFN_TM = 512
FFN_FC = 512
FFN_PAD = ((FFN_DIM + FFN_FC - 1) // FFN_FC) * FFN_FC
FFN_HALO = BF16_ROWS
DIFF_TQ = 256
WIN_TQ = 512

F32 = jnp.float32
BF16 = jnp.bfloat16


def _params(*sem):
    return pltpu.CompilerParams(dimension_semantics=sem, vmem_limit_bytes=VMEM_LIMIT)


def _layer_norm_rows(z, g, b):
    mu = jnp.mean(z, axis=-1, keepdims=True)
    d = z - mu
    var = jnp.mean(d * d, axis=-1, keepdims=True)
    return d * lax.rsqrt(var + LN_EPS) * g + b


def _rope_tables(seq, head_dim, q_scale):
    rot = head_dim // ROPE_FRACTION
    half = rot // 2
    pos = jnp.arange(seq, dtype=F32)
    inv_freq = ROPE_THETA ** (-jnp.arange(0, rot, 2, dtype=F32) / rot)
    ang = pos[:, None] * inv_freq[None, :]
    cos, sin = jnp.cos(ang), jnp.sin(ang)
    pad = head_dim - rot
    c = jnp.concatenate([cos, cos, jnp.ones((seq, pad), F32)], axis=-1)
    a = jnp.concatenate([jnp.zeros((seq, half), F32), sin, jnp.zeros((seq, pad), F32)], axis=-1)
    b = jnp.concatenate([-sin, jnp.zeros((seq, half + pad), F32)], axis=-1)
    k_tab = jnp.stack([jnp.tile(t, (1, LANES // head_dim)) for t in (c, a, b)])
    return jnp.stack([k_tab * q_scale, k_tab])


def _qkv_kernel(x_ref, w_ref, tab_ref, o_ref, xs_ref, *, n_rope_tiles, half):
    j = pl.program_id(1)

    @pl.when(j == 0)
    def _():
        xs_ref[...] = x_ref[...].astype(BF16)

    acc = jnp.dot(xs_ref[...], w_ref[...], preferred_element_type=F32)

    @pl.when(j < n_rope_tiles)
    def _():
        c, a, b = tab_ref[0], tab_ref[1], tab_ref[2]
        for t in range(acc.shape[1] // LANES):
            blk = acc[:, t * LANES:(t + 1) * LANES]
            y = blk * c + pltpu.roll(blk, half, 1) * a + pltpu.roll(blk, LANES - half, 1) * b
            o_ref[:, t * LANES:(t + 1) * LANES] = y.astype(o_ref.dtype)

    @pl.when(j >= n_rope_tiles)
    def _():
        o_ref[...] = acc.astype(o_ref.dtype)


def _qkv_rope(x2d, w, tabs, seq, *, q_cols, rope_cols, half):
    n, d = x2d.shape
    c = w.shape[1]
    tm, tn = QKV_TM, QKV_TN
    assert n % tm == 0 and seq % tm == 0 and c % tn == 0 and q_cols % tn == 0 and rope_cols % tn == 0
    n_q_tiles = q_cols // tn
    seq_tiles = seq // tm
    kern = functools.partial(_qkv_kernel, n_rope_tiles=rope_cols // tn, half=half)
    return pl.pallas_call(
        kern,
        grid=(n // tm, c // tn),
        in_specs=[
            pl.BlockSpec((tm, d), lambda i, j: (i, 0)),
            pl.BlockSpec((d, tn), lambda i, j: (0, j)),
            pl.BlockSpec((None, 3, tm, LANES),
                         lambda i, j: (jnp.where(j >= n_q_tiles, 1, 0), 0, i % seq_tiles, 0)),
        ],
        out_specs=pl.BlockSpec((tm, tn), lambda i, j: (i, j)),
        out_shape=jax.ShapeDtypeStruct((n, c), BF16),
        scratch_shapes=[pltpu.VMEM((tm, d), BF16)],
        compiler_params=_params("parallel", "arbitrary"),
        name="qkv_rope",
    )(x2d, w, tabs)


def _diff_attn_kernel(lam_ref, g_ref, q_ref, k_ref, v_ref, o_ref, *, tq, lambda_init):
    seq = q_ref.shape[0]
    d = DIFF_HEAD_DIM
    lf = lam_ref[...]
    lam = (jnp.exp(jnp.sum(lf[0:1] * lf[1:2], axis=-1, keepdims=True))
           - jnp.exp(jnp.sum(lf[2:3] * lf[3:4], axis=-1, keepdims=True)) + lambda_init)
    gain = g_ref[...] * (1.0 - lambda_init)
    k = k_ref[...]
    v = v_ref[...]
    lane = lax.broadcasted_iota(jnp.int32, (tq, 2 * d), 1)
    zero = jnp.zeros((tq, 2 * d), BF16)

    def body(i, carry):
        r0 = pl.multiple_of(i * tq, tq)
        q = q_ref[pl.ds(r0, tq), :]
        qq = jnp.concatenate([jnp.where(lane < d, q, zero), jnp.where(lane >= d, q, zero)], axis=0)
        s = lax.dot_general(qq, k, (((1,), (1,)), ((), ())), preferred_element_type=F32)
        m = jnp.max(s, axis=-1, keepdims=True)
        e = jnp.exp(s - m)
        r = 1.0 / jnp.sum(e, axis=-1, keepdims=True)
        a = e[:tq] * r[:tq] - e[tq:] * (r[tq:] * lam)
        o = jnp.dot(a.astype(BF16), v, preferred_element_type=F32)
        o = o * lax.rsqrt(jnp.mean(o * o, axis=-1, keepdims=True) + LN_EPS) * gain
        o_ref[pl.ds(r0, tq), :] = o.astype(o_ref.dtype)
        return carry

    lax.fori_loop(0, seq // tq, body, 0)


def _diff_attention(qkv, lam, subln_g, lambda_init):
    b, seq, _ = qkv.shape
    hw = 2 * DIFF_HEAD_DIM
    assert hw == LANES and seq % DIFF_TQ == 0
    k_off = 2 * DIFF_HEADS * DIFF_HEAD_DIM // hw
    v_off = 2 * k_off
    kern = functools.partial(_diff_attn_kernel, tq=DIFF_TQ, lambda_init=lambda_init)
    return pl.pallas_call(
        kern,
        grid=(b, DIFF_HEADS),
        in_specs=[
            pl.BlockSpec((4, DIFF_HEAD_DIM), lambda bi, h: (0, 0)),
            pl.BlockSpec((1, hw), lambda bi, h: (0, 0)),
            pl.BlockSpec((None, seq, hw), lambda bi, h: (bi, 0, h)),
            pl.BlockSpec((None, seq, hw), lambda bi, h: (bi, 0, k_off + h)),
            pl.BlockSpec((None, seq, hw), lambda bi, h: (bi, 0, v_off + h)),
        ],
        out_specs=pl.BlockSpec((None, seq, hw), lambda bi, h: (bi, 0, h)),
        out_shape=jax.ShapeDtypeStruct((b, seq, D_MODEL), BF16),
        compiler_params=_params("parallel", "parallel"),
        name="diff_attn",
    )(lam, subln_g.reshape(1, hw), qkv, qkv, qkv)


def _win_attn_kernel(sink_ref, q_ref, kp_ref, kc_ref, kn_ref, vp_ref, vc_ref, vn_ref, o_ref, *, seq):
    w, hd, rep = WINDOW, WIN_HEAD_DIM, WIN_REP
    grp = pl.program_id(1)
    n = pl.program_id(2)
    nsub = q_ref.shape[0] // w
    kband = jnp.concatenate([kp_ref[...], kc_ref[...], kn_ref[...]], axis=0)
    vband = jnp.concatenate([vp_ref[...], vc_ref[...], vn_ref[...]], axis=0)
    sink = jnp.concatenate(
        [jnp.full((w, 1), sink_ref[grp * rep + r], F32) for r in range(rep)], axis=0)
    qrow = lax.broadcasted_iota(jnp.int32, (rep * w, 3 * w), 0) & (w - 1)
    kcol = lax.broadcasted_iota(jnp.int32, (rep * w, 3 * w), 1)
    rel = kcol - (qrow + w)
    in_band = (rel >= -w) & (rel <= w)
    for sb in range(nsub):
        q4 = jnp.concatenate(
            [q_ref[sb * w:(sb + 1) * w, r * hd:(r + 1) * hd] for r in range(rep)], axis=0)
        kb = kband[sb * w:(sb + 3) * w]
        vb = vband[sb * w:(sb + 3) * w]
        s = lax.dot_general(q4, kb, (((1,), (1,)), ((), ())), preferred_element_type=F32)
        kpos = (n * nsub + (sb - 1)) * w + kcol
        valid = in_band & (kpos >= 0) & (kpos < seq)
        s = jnp.where(valid, s, -jnp.inf)
        m = jnp.maximum(jnp.max(s, axis=-1, keepdims=True), sink)
        e = jnp.exp(s - m)
        r_den = 1.0 / (jnp.sum(e, axis=-1, keepdims=True) + jnp.exp(sink - m))
        o4 = jnp.dot((e * r_den).astype(BF16), vb, preferred_element_type=F32)
        for r in range(rep):
            o_ref[sb * w:(sb + 1) * w, r * hd:(r + 1) * hd] = o4[r * w:(r + 1) * w].astype(o_ref.dtype)


def _win_attention(qkv, sink):
    b, seq, _ = qkv.shape
    w, hd = WINDOW, WIN_HEAD_DIM
    assert hd == LANES and seq % WIN_TQ == 0
    nsub = WIN_TQ // w
    nblk = seq // w
    gw = WIN_REP * hd
    k_off = WIN_Q_HEADS
    v_off = WIN_Q_HEADS + WIN_KV_HEADS

    def halo_prev(off):
        return pl.BlockSpec((None, w, hd), lambda bi, g, n: (bi, jnp.maximum(n * nsub - 1, 0), off + g))

    def halo_next(off):
        return pl.BlockSpec((None, w, hd), lambda bi, g, n: (bi, jnp.minimum((n + 1) * nsub, nblk - 1), off + g))

    def centre(off):
        return pl.BlockSpec((None, WIN_TQ, hd), lambda bi, g, n: (bi, n, off + g))

    return pl.pallas_call(
        functools.partial(_win_attn_kernel, seq=seq),
        grid=(b, WIN_KV_HEADS, seq // WIN_TQ),
        in_specs=[
            pl.BlockSpec(memory_space=pltpu.SMEM),
            pl.BlockSpec((None, WIN_TQ, gw), lambda bi, g, n: (bi, n, g)),
            halo_prev(k_off), centre(k_off), halo_next(k_off),
            halo_prev(v_off), centre(v_off), halo_next(v_off),
        ],
        out_specs=pl.BlockSpec((None, WIN_TQ, gw), lambda bi, g, n: (bi, n, g)),
        out_shape=jax.ShapeDtypeStruct((b, seq, D_MODEL), BF16),
        compiler_params=_params("parallel", "parallel", "parallel"),
        name="win_attn",
    )(sink, qkv, qkv, qkv, qkv, qkv, qkv, qkv)


def _proj_ln_kernel(a_ref, w_ref, x_ref, g_ref, b_ref, o_ref):
    y = jnp.dot(a_ref[...], w_ref[...], preferred_element_type=F32)
    o_ref[...] = _layer_norm_rows(ALPHA * x_ref[...] + y, g_ref[...], b_ref[...])


def _proj_res_ln(a2d, w, x2d, g, b):
    n, d = x2d.shape
    tm = PROJ_TM
    assert n % tm == 0
    row = pl.BlockSpec((tm, d), lambda i: (i, 0))
    vec = pl.BlockSpec((1, d), lambda i: (0, 0))
    return pl.pallas_call(
        _proj_ln_kernel,
        grid=(n // tm,),
        in_specs=[row, pl.BlockSpec((d, d), lambda i: (0, 0)), row, vec, vec],
        out_specs=row,
        out_shape=jax.ShapeDtypeStruct((n, d), F32),
        compiler_params=_params("parallel"),
        name="proj_res_ln",
    )(a2d, w, x2d, g.reshape(1, d), b.reshape(1, d))


def _ffn_kernel(xp_ref, x_ref, xn_ref, wg_ref, wu_ref, cwg_ref, cwu_ref, cbg_ref, cbu_ref, wd_ref,
                g_ref, b_ref, o_ref, xs_ref, *, tiles_per_seq):
    i = pl.program_id(0)
    c = pl.program_id(1)
    tm = x_ref.shape[0]
    hl = FFN_HALO

    @pl.when(c == 0)
    def _():
        t = i % tiles_per_seq
        xs_ref[0:hl, :] = jnp.where(t == 0, 0.0, xp_ref[...]).astype(BF16)
        xs_ref[hl:hl + tm, :] = x_ref[...].astype(BF16)
        xs_ref[hl + tm:, :] = jnp.where(t == tiles_per_seq - 1, 0.0, xn_ref[...]).astype(BF16)
        o_ref[...] = ALPHA * x_ref[...]

    xs = xs_ref[...]

    def conv(w_ref, cw_ref, cb_ref):
        h = jnp.dot(xs, w_ref[...], preferred_element_type=F32)
        cw = cw_ref[...]
        return (h[hl - 1:hl - 1 + tm] * cw[0:1] + h[hl:hl + tm] * cw[1:2]
                + h[hl + 1:hl + 1 + tm] * cw[2:3] + cb_ref[...])

    gate = conv(wg_ref, cwg_ref, cbg_ref)
    up = conv(wu_ref, cwu_ref, cbu_ref)
    act = gate * (1.0 / (1.0 + jnp.exp(-gate))) * up
    o_ref[...] += jnp.dot(act.astype(BF16), wd_ref[...], preferred_element_type=F32)

    @pl.when(c == pl.num_programs(1) - 1)
    def _():
        o_ref[...] = _layer_norm_rows(o_ref[...], g_ref[...], b_ref[...])


def _ffn_res_ln(x2d, seq, w_up, conv_w, conv_b, w_down, g, b):
    n, d = x2d.shape
    tm, fc, hl = FFN_TM, FFN_FC, FFN_HALO
    assert n % tm == 0 and seq % tm == 0 and tm % hl == 0
    nc = FFN_PAD // fc
    hb = tm // hl
    kern = functools.partial(_ffn_kernel, tiles_per_seq=seq // tm)
    vec = pl.BlockSpec((1, d), lambda i, c: (0, 0))
    return pl.pallas_call(
        kern,
        grid=(n // tm, nc),
        in_specs=[
            pl.BlockSpec((hl, d), lambda i, c: (jnp.maximum(i * hb - 1, 0), 0)),
            pl.BlockSpec((tm, d), lambda i, c: (i, 0)),
            pl.BlockSpec((hl, d), lambda i, c: (jnp.minimum((i + 1) * hb, n // hl - 1), 0)),
            pl.BlockSpec((d, fc), lambda i, c: (0, c)),
            pl.BlockSpec((d, fc), lambda i, c: (0, nc + c)),
            pl.BlockSpec((None, 3, fc), lambda i, c: (0, 0, c)),
            pl.BlockSpec((None, 3, fc), lambda i, c: (1, 0, c)),
            pl.BlockSpec((None, 1, fc), lambda i, c: (0, 0, c)),
            pl.BlockSpec((None, 1, fc), lambda i, c: (1, 0, c)),
            pl.BlockSpec((fc, d), lambda i, c: (c, 0)),
            vec, vec,
        ],
        out_specs=pl.BlockSpec((tm, d), lambda i, c: (i, 0)),
        out_shape=jax.ShapeDtypeStruct((n, d), F32),
        scratch_shapes=[pltpu.VMEM((tm + 2 * hl, d), BF16)],
        compiler_params=_params("parallel", "arbitrary"),
        name="ffn_res_ln",
    )(x2d, x2d, x2d, w_up, w_up, conv_w, conv_w, conv_b, conv_b, w_down, g.reshape(1, d), b.reshape(1, d))


def _pad_ffn(t):
    return jnp.pad(t, [(0, 0)] * (t.ndim - 1) + [(0, FFN_PAD - FFN_DIM)])


def _prep_weights(diff_w_in, diff_w_out, win_w_in, win_w_out, ffn_w_up, ffn_conv_w, ffn_conv_b, ffn_w_down):
    up = _pad_ffn(ffn_w_up.reshape(DEPTH, D_MODEL, 2, FFN_DIM)).reshape(DEPTH, D_MODEL, 2 * FFN_PAD)
    cw = _pad_ffn(ffn_conv_w.reshape(DEPTH, 3, 2, FFN_DIM).transpose(0, 2, 1, 3))
    cb = _pad_ffn(ffn_conv_b.reshape(DEPTH, 2, 1, FFN_DIM))
    down = jnp.pad(ffn_w_down, ((0, 0), (0, FFN_PAD - FFN_DIM), (0, 0)))
    return dict(diff_w_in=diff_w_in.astype(BF16), diff_w_out=diff_w_out.astype(BF16),
                win_w_in=win_w_in.astype(BF16), win_w_out=win_w_out.astype(BF16),
                up=up.astype(BF16), cw=cw, cb=cb, down=down.astype(BF16))


def _trunk(x, wts, diff_lam, diff_subln_g, win_sink, ln_mix_g, ln_mix_b, ln_ffn_g, ln_ffn_b):
    b, seq, d = x.shape
    n = b * seq
    x2d = x.reshape(n, d)
    diff_tabs = _rope_tables(seq, DIFF_HEAD_DIM, DIFF_HEAD_DIM ** -0.5)
    win_tabs = _rope_tables(seq, WIN_HEAD_DIM, WIN_HEAD_DIM ** -0.5)
    for i in range(DEPTH):
        j = i // N_MIXERS
        if i % N_MIXERS == 0:
            lambda_init = 0.8 - 0.6 * math.exp(-0.3 * i)
            qk = 2 * DIFF_HEADS * DIFF_HEAD_DIM
            qkv = _qkv_rope(x2d, wts["diff_w_in"][j], diff_tabs, seq, q_cols=qk, rope_cols=2 * qk,
                            half=DIFF_HEAD_DIM // ROPE_FRACTION // 2)
            att = _diff_attention(qkv.reshape(b, seq, DIFF_IN), diff_lam[j], diff_subln_g[j], lambda_init)
            w_out = wts["diff_w_out"][j]
        else:
            qc = WIN_Q_HEADS * WIN_HEAD_DIM
            qkv = _qkv_rope(x2d, wts["win_w_in"][j], win_tabs, seq, q_cols=qc,
                            rope_cols=qc + WIN_KV_HEADS * WIN_HEAD_DIM,
                            half=WIN_HEAD_DIM // ROPE_FRACTION // 2)
            att = _win_attention(qkv.reshape(b, seq, WIN_IN), win_sink[j])
            w_out = wts["win_w_out"][j]
        x2d = _proj_res_ln(att.reshape(n, d), w_out, x2d, ln_mix_g[i], ln_mix_b[i])
        x2d = _ffn_res_ln(x2d, seq, wts["up"][i], wts["cw"][i], wts["cb"][i], wts["down"][i],
                          ln_ffn_g[i], ln_ffn_b[i])
    return x2d.reshape(b, seq, d)


def kernel(x_prompt, x_sample, diff_w_in, diff_lam, diff_subln_g, diff_w_out, win_w_in, win_sink, win_w_out,
           ln_mix_g, ln_mix_b, ffn_w_up, ffn_conv_w, ffn_conv_b, ffn_w_down, ln_ffn_g, ln_ffn_b):
    wts = _prep_weights(diff_w_in, diff_w_out, win_w_in, win_w_out, ffn_w_up, ffn_conv_w, ffn_conv_b, ffn_w_down)
    rest = (diff_lam, diff_subln_g, win_sink, ln_mix_g, ln_mix_b, ln_ffn_g, ln_ffn_b)
    return (_trunk(x_prompt, wts, *rest), _trunk(x_sample, wts, *rest))
```

```python
import functools
import math

import jax
import jax.numpy as jnp
from jax import lax
from jax.experimental import pallas as pl
from jax.experimental.pallas import tpu as pltpu

D_MODEL = 2048
DEPTH = 4
N_MIXERS = 2
DIFF_HEAD_DIM = 64
DIFF_HEADS = D_MODEL // (2 * DIFF_HEAD_DIM)
DIFF_IN = 6 * DIFF_HEADS * DIFF_HEAD_DIM
WIN_HEAD_DIM = 128
WIN_Q_HEADS = D_MODEL // WIN_HEAD_DIM
WIN_KV_HEADS = 4
WIN_REP = WIN_Q_HEADS // WIN_KV_HEADS
WIN_IN = (WIN_Q_HEADS + 2 * WIN_KV_HEADS) * WIN_HEAD_DIM
WINDOW = 128
FFN_DIM = ((8 * D_MODEL // 3 + 127) // 128) * 128
ROPE_THETA = 500000.0
ROPE_FRACTION = 4
ALPHA = (2 * DEPTH) ** 0.25
LN_EPS = 1e-5
LOG2E = math.log2(math.e)

LANES = 128
HALF_LANES = LANES // 2
BF16_ROWS = 16
MXU_COLS = 256
VMEM_LIMIT = 56 * 1024 * 1024

QKV_TM = 1024
QKV_TN = 512
PROJ_TM = 512
FFN_TM = 512
FFN_FC = 512
FFN_NC = (FFN_DIM + FFN_FC - 1) // FFN_FC
FFN_PAD = FFN_NC * FFN_FC
FFN_HALO = BF16_ROWS
DIFF_TQ = 256
WIN_TQ = 512

F32 = jnp.float32
BF16 = jnp.bfloat16


def _params(*sem):
    return pltpu.CompilerParams(dimension_semantics=sem, vmem_limit_bytes=VMEM_LIMIT)


def _layer_norm_rows(z, g, b):
    mu = jnp.mean(z, axis=-1, keepdims=True)
    d = z - mu
    var = jnp.mean(d * d, axis=-1, keepdims=True)
    return d * lax.rsqrt(var + LN_EPS) * g + b


def _head_block_perm(head_dim):
    rot = head_dim // ROPE_FRACTION
    half = rot // 2
    x1, x2, rest = [], [], []
    for m in range(LANES // head_dim):
        base = m * head_dim
        x1 += range(base, base + half)
        x2 += range(base + half, base + rot)
        rest += range(base + rot, base + head_dim)
    n_low = HALF_LANES - len(x1)
    return x1 + rest[:n_low] + x2 + rest[n_low:]


def _permute_head_blocks(w, n_blocks, head_dim):
    perm = _head_block_perm(head_dim)
    runs, start = [], 0
    for i in range(1, LANES + 1):
        if i == LANES or perm[i] != perm[i - 1] + 1:
            runs.append((perm[start], perm[i - 1] + 1))
            start = i
    lead = w.shape[:-1]
    blk = w[..., :n_blocks * LANES].reshape(*lead, n_blocks, LANES)
    blk = jnp.concatenate([blk[..., a:b] for a, b in runs], axis=-1)
    return jnp.concatenate([blk.reshape(*lead, n_blocks * LANES), w[..., n_blocks * LANES:]], axis=-1)


def _rope_tables(seq, head_dim, q_scale):
    rot = head_dim // ROPE_FRACTION
    reps = LANES // head_dim
    pos = jnp.arange(seq, dtype=F32)
    inv_freq = ROPE_THETA ** (-jnp.arange(0, rot, 2, dtype=F32) / rot)
    ang = pos[:, None] * inv_freq[None, :]
    cos, sin = jnp.tile(jnp.cos(ang), (1, reps)), jnp.tile(jnp.sin(ang), (1, reps))
    pad = HALF_LANES - cos.shape[1]
    ones, zeros = jnp.ones((seq, pad), F32), jnp.zeros((seq, pad), F32)
    c = jnp.concatenate([cos, ones, cos, ones], axis=-1)
    s = jnp.concatenate([-sin, zeros, sin, zeros], axis=-1)
    k_tab = jnp.stack([c, s])
    ident = jnp.stack([jnp.ones((seq, LANES), F32), jnp.zeros((seq, LANES), F32)])
    return jnp.stack([k_tab * q_scale, k_tab, ident])


def _qkv_kernel(x_ref, w_ref, tab_ref, o_ref, xs_ref):
    @pl.when(pl.program_id(1) == 0)
    def _():
        xs_ref[...] = x_ref[...].astype(BF16)

    hm = x_ref.shape[0] // 2
    for r0 in (0, hm):
        acc = jnp.dot(xs_ref[r0:r0 + hm, :], w_ref[...], preferred_element_type=F32)
        c, s = tab_ref[0, r0:r0 + hm, :], tab_ref[1, r0:r0 + hm, :]
        for t in range(acc.shape[1] // LANES):
            blk = acc[:, t * LANES:(t + 1) * LANES]
            y = blk * c + pltpu.roll(blk, HALF_LANES, 1) * s
            o_ref[r0:r0 + hm, t * LANES:(t + 1) * LANES] = y.astype(o_ref.dtype)


def _qkv_rope(x2d, w, layer, tabs, seq, *, q_cols, rope_cols):
    n, d = x2d.shape
    c = w.shape[-1]
    tm, tn = QKV_TM, QKV_TN
    assert n % tm == 0 and seq % tm == 0 and c % tn == 0 and q_cols % tn == 0 and rope_cols % tn == 0
    n_q_tiles = q_cols // tn
    n_rope_tiles = rope_cols // tn
    seq_tiles = seq // tm

    def tab_index(i, j):
        return ((j >= n_q_tiles).astype(jnp.int32) + (j >= n_rope_tiles).astype(jnp.int32), 0, i % seq_tiles, 0)

    return pl.pallas_call(
        _qkv_kernel,
        grid=(n // tm, c // tn),
        in_specs=[
            pl.BlockSpec((tm, d), lambda i, j: (i, 0)),
            pl.BlockSpec((None, d, tn), lambda i, j: (layer, 0, j)),
            pl.BlockSpec((None, 2, tm, LANES), tab_index),
        ],
        out_specs=pl.BlockSpec((tm, tn), lambda i, j: (i, j)),
        out_shape=jax.ShapeDtypeStruct((n, c), BF16),
        scratch_shapes=[pltpu.VMEM((tm, d), BF16)],
        compiler_params=_params("parallel", "arbitrary"),
        name="qkv_rope",
    )(x2d, w, tabs)


def _diff_attn_kernel(lam_ref, g_ref, q_ref, k_ref, v_ref, o_ref, s0_ref, s1_ref, e0_ref, e1_ref, vx_ref, *,
                      tq, lambda_init):
    seq = q_ref.shape[0]
    nq = seq // tq
    d = DIFF_HEAD_DIM
    lf = lam_ref[...]
    lam = (jnp.exp(jnp.sum(lf[0:1] * lf[1:2], axis=-1, keepdims=True))
           - jnp.exp(jnp.sum(lf[2:3] * lf[3:4], axis=-1, keepdims=True)) + lambda_init)
    gain = g_ref[...] * (1.0 - lambda_init)
    lane = lax.broadcasted_iota(jnp.int32, (tq, 2 * d), 1)
    perm = _head_block_perm(d)
    map1_start_low = perm.index(d)
    map1_start_high = HALF_LANES + map1_start_low
    in_map0 = (lane < map1_start_low) | ((lane >= 2 * map1_start_low) & (lane < map1_start_high))
    zero = jnp.zeros((tq, 2 * d), BF16)
    s_refs, e_refs = (s0_ref, s1_ref), (e0_ref, e1_ref)
    vx_ref[:, :2 * d] = v_ref[...]
    vx_ref[:, 2 * d:] = jnp.ones((seq, 2 * d), BF16)

    def rows(i):
        return pl.ds(pl.multiple_of(i * tq, tq), tq)

    def scores(i, slot):
        q = q_ref[rows(i), :]
        qq = jnp.concatenate([jnp.where(in_map0, q, zero), jnp.where(in_map0, zero, q)], axis=0)
        s_refs[slot][...] = lax.dot_general(qq, k_ref[...], (((1,), (1,)), ((), ())),
                                            preferred_element_type=F32)

    def combine(slot):
        s = s_refs[slot][...]
        e_refs[slot][...] = jnp.exp2(s - jnp.max(s, axis=-1, keepdims=True)).astype(BF16)

    def weighted_values(i, slot):
        ov = jnp.dot(e_refs[slot][...], vx_ref[...], preferred_element_type=F32)
        r1 = 1.0 / ov[:tq, 2 * d:2 * d + 1]
        r2 = lam * (1.0 / ov[tq:, 2 * d:2 * d + 1])
        o = ov[:tq, :2 * d] * r1 - ov[tq:, :2 * d] * r2
        o = o * lax.rsqrt(jnp.mean(o * o, axis=-1, keepdims=True) + LN_EPS) * gain
        o_ref[rows(i), :] = o.astype(o_ref.dtype)

    scores(0, 0)
    scores(1, 1)
    combine(0)

    def pair(j, carry):
        i = 2 * j + 1
        weighted_values(i - 1, 0)
        scores(i + 1, 0)
        combine(1)
        weighted_values(i, 1)
        scores(i + 2, 1)
        combine(0)
        return carry

    lax.fori_loop(0, nq // 2 - 1, pair, 0)
    weighted_values(nq - 2, 0)
    combine(1)
    weighted_values(nq - 1, 1)


def _diff_attention(qkv, lam, subln_g, layer, lambda_init):
    b, seq, _ = qkv.shape
    hw = 2 * DIFF_HEAD_DIM
    assert hw == LANES and seq % (2 * DIFF_TQ) == 0
    perm = _head_block_perm(DIFF_HEAD_DIM)
    m1 = perm.index(DIFF_HEAD_DIM)
    assert all((p < DIFF_HEAD_DIM) == (l < m1 or 2 * m1 <= l < HALF_LANES + m1) for l, p in enumerate(perm))
    k_off = 2 * DIFF_HEADS * DIFF_HEAD_DIM // hw
    v_off = 2 * k_off
    kern = functools.partial(_diff_attn_kernel, tq=DIFF_TQ, lambda_init=lambda_init)
    return pl.pallas_call(
        kern,
        grid=(b, DIFF_HEADS),
        in_specs=[
            pl.BlockSpec((None, 4, DIFF_HEAD_DIM), lambda bi, h: (layer, 0, 0)),
            pl.BlockSpec((None, 1, hw), lambda bi, h: (layer, 0, 0)),
            pl.BlockSpec((None, seq, hw), lambda bi, h: (bi, 0, h)),
            pl.BlockSpec((None, seq, hw), lambda bi, h: (bi, 0, k_off + h)),
            pl.BlockSpec((None, seq, hw), lambda bi, h: (bi, 0, v_off + h)),
        ],
        out_specs=pl.BlockSpec((None, seq, hw), lambda bi, h: (bi, 0, h)),
        out_shape=jax.ShapeDtypeStruct((b, seq, D_MODEL), BF16),
        scratch_shapes=[pltpu.VMEM((2 * DIFF_TQ, seq), F32), pltpu.VMEM((2 * DIFF_TQ, seq), F32),
                        pltpu.VMEM((2 * DIFF_TQ, seq), BF16), pltpu.VMEM((2 * DIFF_TQ, seq), BF16),
                        pltpu.VMEM((seq, 2 * hw), BF16)],
        compiler_params=_params("parallel", "parallel"),
        name="diff_attn",
    )(lam, subln_g.reshape(-1, 1, hw), qkv, qkv, qkv)


def _win_attn_kernel(sink_ref, q_ref, kp_ref, kc_ref, kn_ref, vp_ref, vc_ref, vn_ref, o_ref, *, layer, seq):
    w, hd, rep = WINDOW, WIN_HEAD_DIM, WIN_REP
    grp = pl.program_id(1)
    n = pl.program_id(2)
    nsub = q_ref.shape[0] // w
    kband = jnp.concatenate([kp_ref[...], kc_ref[...], kn_ref[...]], axis=0)
    vband = jnp.concatenate([vp_ref[...], vc_ref[...], vn_ref[...]], axis=0)
    vband = jnp.concatenate([vband, jnp.ones(vband.shape, BF16)], axis=1)
    sink = jnp.concatenate(
        [jnp.full((w, hd), sink_ref[layer, grp * rep + r] * LOG2E, F32) for r in range(rep)], axis=0)
    qrow = lax.broadcasted_iota(jnp.int32, (rep * w, 3 * w), 0) & (w - 1)
    kcol = lax.broadcasted_iota(jnp.int32, (rep * w, 3 * w), 1)
    rel = kcol - qrow
    band_bias = jnp.where((rel >= 0) & (rel <= 2 * w), 0.0, -jnp.inf).astype(F32)
    kcol_row = lax.broadcasted_iota(jnp.int32, (1, 3 * w), 1)
    for sb in range(nsub):
        q4 = jnp.concatenate(
            [q_ref[sb * w:(sb + 1) * w, r * hd:(r + 1) * hd] for r in range(rep)], axis=0)
        kb = kband[sb * w:(sb + 3) * w]
        vb = vband[sb * w:(sb + 3) * w]
        s = lax.dot_general(q4, kb, (((1,), (1,)), ((), ())), preferred_element_type=F32)
        bias = band_bias
        if sb == 0 or sb == nsub - 1:
            kpos = (n * nsub + (sb - 1)) * w + kcol_row
            bias = bias + jnp.where((kpos >= 0) & (kpos < seq), 0.0, -jnp.inf)
        s = s + bias
        m = jnp.maximum(jnp.broadcast_to(jnp.max(s, axis=-1, keepdims=True), sink.shape), sink)
        e = jnp.concatenate([jnp.exp2(s[:, t * hd:(t + 1) * hd] - m) for t in range(3 * w // hd)], axis=1)
        ov = jnp.dot(e.astype(BF16), vb, preferred_element_type=F32)
        o4 = ov[:, :hd] * (1.0 / (ov[:, hd:] + jnp.exp2(sink - m)))
        for r in range(rep):
            o_ref[sb * w:(sb + 1) * w, r * hd:(r + 1) * hd] = o4[r * w:(r + 1) * w].astype(o_ref.dtype)


def _win_attention(qkv, sink, layer):
    b, seq, _ = qkv.shape
    w, hd = WINDOW, WIN_HEAD_DIM
    assert hd == LANES and seq % WIN_TQ == 0
    nsub = WIN_TQ // w
    nblk = seq // w
    gw = WIN_REP * hd
    k_off = WIN_Q_HEADS
    v_off = WIN_Q_HEADS + WIN_KV_HEADS

    def halo_prev(off):
        return pl.BlockSpec((None, w, hd), lambda bi, g, n: (bi, jnp.maximum(n * nsub - 1, 0), off + g))

    def halo_next(off):
        return pl.BlockSpec((None, w, hd), lambda bi, g, n: (bi, jnp.minimum((n + 1) * nsub, nblk - 1), off + g))

    def centre(off):
        return pl.BlockSpec((None, WIN_TQ, hd), lambda bi, g, n: (bi, n, off + g))

    return pl.pallas_call(
        functools.partial(_win_attn_kernel, layer=layer, seq=seq),
        grid=(b, WIN_KV_HEADS, seq // WIN_TQ),
        in_specs=[
            pl.BlockSpec(memory_space=pltpu.SMEM),
            pl.BlockSpec((None, WIN_TQ, gw), lambda bi, g, n: (bi, n, g)),
            halo_prev(k_off), centre(k_off), halo_next(k_off),
            halo_prev(v_off), centre(v_off), halo_next(v_off),
        ],
        out_specs=pl.BlockSpec((None, WIN_TQ, gw), lambda bi, g, n: (bi, n, g)),
        out_shape=jax.ShapeDtypeStruct((b, seq, D_MODEL), BF16),
        compiler_params=_params("parallel", "parallel", "parallel"),
        name="win_attn",
    )(sink, qkv, qkv, qkv, qkv, qkv, qkv, qkv)


def _proj_ln_kernel(a_ref, w_ref, x_ref, g_ref, b_ref, o_ref):
    y = jnp.dot(a_ref[...], w_ref[...], preferred_element_type=F32)
    o_ref[...] = _layer_norm_rows(ALPHA * x_ref[...] + y, g_ref[...], b_ref[...])


def _proj_res_ln(a2d, w, w_layer, x2d, g, b, layer):
    n, d = x2d.shape
    tm = PROJ_TM
    assert n % tm == 0
    row = pl.BlockSpec((tm, d), lambda i: (i, 0))
    vec = pl.BlockSpec((None, 1, d), lambda i: (layer, 0, 0))
    return pl.pallas_call(
        _proj_ln_kernel,
        grid=(n // tm,),
        in_specs=[row, pl.BlockSpec((None, d, d), lambda i: (w_layer, 0, 0)), row, vec, vec],
        out_specs=row,
        out_shape=jax.ShapeDtypeStruct((n, d), F32),
        compiler_params=_params("parallel"),
        name="proj_res_ln",
    )(a2d, w, x2d, g, b)


def _ffn_kernel(xp_ref, x_ref, xn_ref, wu_ref, cw_ref, cb_ref, wd_ref, g_ref, b_ref, o_ref, xs_ref,
                h_ref, act0_ref, act1_ref, *, tiles_per_seq):
    i = pl.program_id(0)
    c = pl.program_id(1)
    last = pl.num_programs(1) - 1
    tm = x_ref.shape[0]
    hl = FFN_HALO
    mc = MXU_COLS
    act_refs = (act0_ref, act1_ref)

    lanes_per_slab = mc // LANES

    def up_project():
        h = jnp.dot(xs_ref[...], wu_ref[...], preferred_element_type=F32)
        for j in range(h.shape[1] // LANES):
            h_ref[j] = h[:, j * LANES:(j + 1) * LANES]

    def conv_block(j):
        cols = slice(j * LANES, (j + 1) * LANES)
        return (h_ref[j, hl - 1:hl - 1 + tm, :] * cw_ref[0:1, cols] + h_ref[j, hl:hl + tm, :] * cw_ref[1:2, cols]
                + h_ref[j, hl + 1:hl + 1 + tm, :] * cw_ref[2:3, cols] + cb_ref[:, cols])

    def gate_into(act_ref):
        for k in range(act_ref.shape[1] // LANES):
            slab, off = divmod(k, lanes_per_slab)
            gate = conv_block(2 * slab * lanes_per_slab + off)
            up = conv_block((2 * slab + 1) * lanes_per_slab + off)
            act_ref[:, k * LANES:(k + 1) * LANES] = (gate * (1.0 / (1.0 + jnp.exp(-gate))) * up).astype(BF16)

    def down(slot):
        return jnp.dot(act_refs[slot][...], wd_ref[...], preferred_element_type=F32)

    @pl.when(c == 0)
    def _():
        t = i % tiles_per_seq
        xs_ref[0:hl, :] = jnp.where(t == 0, 0.0, xp_ref[...]).astype(BF16)
        xs_ref[hl:hl + tm, :] = x_ref[...].astype(BF16)
        xs_ref[hl + tm:, :] = jnp.where(t == tiles_per_seq - 1, 0.0, xn_ref[...]).astype(BF16)
        o_ref[...] = ALPHA * x_ref[...]
        up_project()
        gate_into(act_refs[0])

    for parity in (0, 1):
        @pl.when((c > 0) & (c < last) & (lax.rem(c, 2) == parity))
        def _():
            up_project()
            o_ref[...] += down(1 - parity)
            gate_into(act_refs[parity])

    @pl.when(c == last)
    def _():
        z = o_ref[...] + down((FFN_NC - 1) % 2)
        o_ref[...] = _layer_norm_rows(z, g_ref[...], b_ref[...])


def _ffn_res_ln(x2d, seq, w_up, conv_w, conv_b, w_down, g, b, layer):
    n, d = x2d.shape
    tm, fc, hl = FFN_TM, FFN_FC, FFN_HALO
    assert n % tm == 0 and seq % tm == 0 and tm % hl == 0 and fc % MXU_COLS == 0
    hb = tm // hl
    kern = functools.partial(_ffn_kernel, tiles_per_seq=seq // tm)
    vec = pl.BlockSpec((None, 1, d), lambda i, c: (layer, 0, 0))
    nc = FFN_NC

    def up_chunk(i, c):
        return (layer, 0, jnp.minimum(c, nc - 1))

    return pl.pallas_call(
        kern,
        grid=(n // tm, nc + 1),
        in_specs=[
            pl.BlockSpec((hl, d), lambda i, c: (jnp.maximum(i * hb - 1, 0), 0)),
            pl.BlockSpec((tm, d), lambda i, c: (i, 0)),
            pl.BlockSpec((hl, d), lambda i, c: (jnp.minimum((i + 1) * hb, n // hl - 1), 0)),
            pl.BlockSpec((None, d, 2 * fc), up_chunk),
            pl.BlockSpec((None, 3, 2 * fc), up_chunk),
            pl.BlockSpec((None, 1, 2 * fc), up_chunk),
            pl.BlockSpec((None, fc, d), lambda i, c: (layer, jnp.maximum(c - 1, 0), 0)),
            vec, vec,
        ],
        out_specs=pl.BlockSpec((tm, d), lambda i, c: (i, 0)),
        out_shape=jax.ShapeDtypeStruct((n, d), F32),
        scratch_shapes=[pltpu.VMEM((tm + 2 * hl, d), BF16), pltpu.VMEM((2 * fc // LANES, tm + 2 * hl, LANES), F32),
                        pltpu.VMEM((tm, fc), BF16), pltpu.VMEM((tm, fc), BF16)],
        compiler_params=_params("parallel", "arbitrary"),
        name="ffn_res_ln",
    )(x2d, x2d, x2d, w_up, conv_w, conv_b, w_down, g, b)


def _interleave_gate_up(t):
    mc = MXU_COLS
    pieces = []
    for s0 in range(0, FFN_PAD, mc):
        for base in (0, FFN_DIM):
            valid = max(0, min(mc, FFN_DIM - s0))
            if valid:
                pieces.append(t[..., base + s0:base + s0 + valid])
            if valid < mc:
                pieces.append(jnp.zeros(t.shape[:-1] + (mc - valid,), t.dtype))
    return jnp.concatenate(pieces, axis=-1)


def _prep_weights(diff_w_in, diff_w_out, win_w_in, win_w_out, ffn_w_up, ffn_conv_w, ffn_conv_b, ffn_w_down):
    diff_qk_blocks = 4 * DIFF_HEADS * DIFF_HEAD_DIM // LANES
    win_qk_blocks = WIN_Q_HEADS + WIN_KV_HEADS
    return dict(
        diff_w_in=_permute_head_blocks(diff_w_in.astype(BF16), diff_qk_blocks, DIFF_HEAD_DIM),
        diff_w_out=diff_w_out.astype(BF16),
        win_w_in=_permute_head_blocks(win_w_in.astype(BF16), win_qk_blocks, WIN_HEAD_DIM),
        win_w_out=win_w_out.astype(BF16),
        up=_interleave_gate_up(ffn_w_up.astype(BF16)),
        cw=_interleave_gate_up(ffn_conv_w),
        cb=_interleave_gate_up(ffn_conv_b)[:, None, :],
        down=jnp.pad(ffn_w_down.astype(BF16), ((0, 0), (0, FFN_PAD - FFN_DIM), (0, 0))))


def _trunk(x, wts, diff_lam, diff_subln_g, win_sink, ln_mix_g, ln_mix_b, ln_ffn_g, ln_ffn_b):
    b, seq, d = x.shape
    n = b * seq
    x2d = x.reshape(n, d)
    diff_tabs = _rope_tables(seq, DIFF_HEAD_DIM, DIFF_HEAD_DIM ** -0.5 * LOG2E)
    win_tabs = _rope_tables(seq, WIN_HEAD_DIM, WIN_HEAD_DIM ** -0.5 * LOG2E)
    mix_g, mix_b = ln_mix_g[:, None, :], ln_mix_b[:, None, :]
    ffn_g, ffn_b = ln_ffn_g[:, None, :], ln_ffn_b[:, None, :]
    for i in range(DEPTH):
        j = i // N_MIXERS
        if i % N_MIXERS == 0:
            lambda_init = 0.8 - 0.6 * math.exp(-0.3 * i)
            qk = 2 * DIFF_HEADS * DIFF_HEAD_DIM
            qkv = _qkv_rope(x2d, wts["diff_w_in"], j, diff_tabs, seq, q_cols=qk, rope_cols=2 * qk)
            att = _diff_attention(qkv.reshape(b, seq, DIFF_IN), diff_lam, diff_subln_g, j, lambda_init)
            w_out = wts["diff_w_out"]
        else:
            qc = WIN_Q_HEADS * WIN_HEAD_DIM
            qkv = _qkv_rope(x2d, wts["win_w_in"], j, win_tabs, seq, q_cols=qc,
                            rope_cols=qc + WIN_KV_HEADS * WIN_HEAD_DIM)
            att = _win_attention(qkv.reshape(b, seq, WIN_IN), win_sink, j)
            w_out = wts["win_w_out"]
        x2d = _proj_res_ln(att.reshape(n, d), w_out, j, x2d, mix_g, mix_b, i)
        x2d = _ffn_res_ln(x2d, seq, wts["up"], wts["cw"], wts["cb"], wts["down"], ffn_g, ffn_b, i)
    return x2d.reshape(b, seq, d)


def kernel(x_prompt, x_sample, diff_w_in, diff_lam, diff_subln_g, diff_w_out, win_w_in, win_sink, win_w_out,
           ln_mix_g, ln_mix_b, ffn_w_up, ffn_conv_w, ffn_conv_b, ffn_w_down, ln_ffn_g, ln_ffn_b):
    wts = _prep_weights(diff_w_in, diff_w_out, win_w_in, win_w_out, ffn_w_up, ffn_conv_w, ffn_conv_b, ffn_w_down)
    rest = (diff_lam, diff_subln_g, win_sink, ln_mix_g, ln_mix_b, ln_ffn_g, ln_ffn_b)
    return (_trunk(x_prompt, wts, *rest), _trunk(x_sample, wts, *rest))
```

```python
import functools
import math

import jax
import jax.numpy as jnp
from jax import lax
from jax.experimental import pallas as pl
from jax.experimental.pallas import tpu as pltpu

D_MODEL = 2048
DEPTH = 4
N_MIXERS = 2
DIFF_HEAD_DIM = 64
DIFF_HEADS = D_MODEL // (2 * DIFF_HEAD_DIM)
DIFF_IN = 6 * DIFF_HEADS * DIFF_HEAD_DIM
WIN_HEAD_DIM = 128
WIN_Q_HEADS = D_MODEL // WIN_HEAD_DIM
WIN_KV_HEADS = 4
WIN_REP = WIN_Q_HEADS // WIN_KV_HEADS
WIN_IN = (WIN_Q_HEADS + 2 * WIN_KV_HEADS) * WIN_HEAD_DIM
WINDOW = 128
FFN_DIM = ((8 * D_MODEL // 3 + 127) // 128) * 128
ROPE_THETA = 500000.0
ROPE_FRACTION = 4
ALPHA = (2 * DEPTH) ** 0.25
LN_EPS = 1e-5
LOG2E = math.log2(math.e)

LANES = 128
HALF_LANES = LANES // 2
BF16_ROWS = 16
MXU_COLS = 256
VMEM_LIMIT = 56 * 1024 * 1024

QKV_TM = 1024
DIFF_QKV_TN = 1024
WIN_QKV_TN = 512
PROJ_TM = 512
PROJ_SLAB = 128
FFN_TM = 1024
FFN_FC = 512
FFN_NC = (FFN_DIM + FFN_FC - 1) // FFN_FC
FFN_PAD = FFN_NC * FFN_FC
F32_ROWS = 8
FFN_HALO = F32_ROWS
DIFF_TQ = 256
DIFF_HEADS_PER_STEP = 2
WIN_TQ = 1024

F32 = jnp.float32
BF16 = jnp.bfloat16


def _params(*sem):
    return pltpu.CompilerParams(dimension_semantics=sem, vmem_limit_bytes=VMEM_LIMIT)


def _layer_norm_rows(z, g, b):
    mu = jnp.mean(z, axis=-1, keepdims=True)
    d = z - mu
    var = jnp.mean(d * d, axis=-1, keepdims=True)
    return d * lax.rsqrt(var + LN_EPS) * g + b


def _head_block_perm(head_dim):
    rot = head_dim // ROPE_FRACTION
    half = rot // 2
    x1, x2, rest = [], [], []
    for m in range(LANES // head_dim):
        base = m * head_dim
        x1 += range(base, base + half)
        x2 += range(base + half, base + rot)
        rest += range(base + rot, base + head_dim)
    n_low = HALF_LANES - len(x1)
    return x1 + rest[:n_low] + x2 + rest[n_low:]


def _permute_head_blocks(w, n_blocks, head_dim):
    perm = _head_block_perm(head_dim)
    runs, start = [], 0
    for i in range(1, LANES + 1):
        if i == LANES or perm[i] != perm[i - 1] + 1:
            runs.append((perm[start], perm[i - 1] + 1))
            start = i
    lead = w.shape[:-1]
    blk = w[..., :n_blocks * LANES].reshape(*lead, n_blocks, LANES)
    blk = jnp.concatenate([blk[..., a:b] for a, b in runs], axis=-1)
    return jnp.concatenate([blk.reshape(*lead, n_blocks * LANES), w[..., n_blocks * LANES:]], axis=-1)


def _rope_tables(seq, head_dim, q_scale):
    rot = head_dim // ROPE_FRACTION
    reps = LANES // head_dim
    pos = jnp.arange(seq, dtype=F32)
    inv_freq = ROPE_THETA ** (-jnp.arange(0, rot, 2, dtype=F32) / rot)
    ang = pos[:, None] * inv_freq[None, :]
    cos, sin = jnp.tile(jnp.cos(ang), (1, reps)), jnp.tile(jnp.sin(ang), (1, reps))
    pad = HALF_LANES - cos.shape[1]
    ones, zeros = jnp.ones((seq, pad), F32), jnp.zeros((seq, pad), F32)
    c = jnp.concatenate([cos, ones, cos, ones], axis=-1)
    s = jnp.concatenate([-sin, zeros, sin, zeros], axis=-1)
    k_tab = jnp.stack([c, s])
    ident = jnp.stack([jnp.ones((seq, LANES), F32), jnp.zeros((seq, LANES), F32)])
    return jnp.stack([k_tab * q_scale, k_tab, ident])


def _qkv_kernel(x_ref, w_ref, tab_ref, o_ref, xs_ref):
    @pl.when(pl.program_id(1) == 0)
    def _():
        xs_ref[...] = x_ref[...].astype(BF16)

    hm = x_ref.shape[0] // 2
    for r0 in (0, hm):
        acc = jnp.dot(xs_ref[r0:r0 + hm, :], w_ref[...], preferred_element_type=F32)
        c, s = tab_ref[0, r0:r0 + hm, :], tab_ref[1, r0:r0 + hm, :]
        for t in range(acc.shape[1] // LANES):
            blk = acc[:, t * LANES:(t + 1) * LANES]
            y = blk * c + pltpu.roll(blk, HALF_LANES, 1) * s
            o_ref[r0:r0 + hm, t * LANES:(t + 1) * LANES] = y.astype(o_ref.dtype)


def _qkv_rope(x2d, w, layer, tabs, seq, *, q_cols, rope_cols):
    n, d = x2d.shape
    tm, tn = QKV_TM, w.shape[3]
    assert w.shape[2] == d
    c = w.shape[1] * tn
    assert n % tm == 0 and seq % tm == 0 and q_cols % tn == 0 and rope_cols % tn == 0
    n_q_tiles = q_cols // tn
    n_rope_tiles = rope_cols // tn
    seq_tiles = seq // tm

    def tab_index(i, j):
        return ((j >= n_q_tiles).astype(jnp.int32) + (j >= n_rope_tiles).astype(jnp.int32), 0, i % seq_tiles, 0)

    return pl.pallas_call(
        _qkv_kernel,
        grid=(n // tm, c // tn),
        in_specs=[
            pl.BlockSpec((tm, d), lambda i, j: (i, 0)),
            pl.BlockSpec((None, None, d, tn), lambda i, j: (layer, j, 0, 0)),
            pl.BlockSpec((None, 2, tm, LANES), tab_index),
        ],
        out_specs=pl.BlockSpec((tm, tn), lambda i, j: (i, j)),
        out_shape=jax.ShapeDtypeStruct((n, c), BF16),
        scratch_shapes=[pltpu.VMEM((tm, d), BF16)],
        compiler_params=_params("parallel", "arbitrary"),
        name="qkv_rope",
    )(x2d, w, tabs)


def _diff_attn_kernel(lam_ref, g_ref, q_ref, k_ref, v_ref, o_ref, s0_ref, s1_ref, e0_ref, e1_ref, vx_ref, *,
                      tq, lambda_init):
    seq = q_ref.shape[0]
    nq = seq // tq
    d = DIFF_HEAD_DIM
    lf = lam_ref[...]
    lam = (jnp.exp(jnp.sum(lf[0:1] * lf[1:2], axis=-1, keepdims=True))
           - jnp.exp(jnp.sum(lf[2:3] * lf[3:4], axis=-1, keepdims=True)) + lambda_init)
    gain = g_ref[...] * (1.0 - lambda_init)
    lane = lax.broadcasted_iota(jnp.int32, (tq, 2 * d), 1)
    perm = _head_block_perm(d)
    map1_start_low = perm.index(d)
    map1_start_high = HALF_LANES + map1_start_low
    in_map0 = (lane < map1_start_low) | ((lane >= 2 * map1_start_low) & (lane < map1_start_high))
    zero = jnp.zeros((tq, 2 * d), BF16)
    s_refs, e_refs = (s0_ref, s1_ref), (e0_ref, e1_ref)

    def rows(i):
        return pl.ds(pl.multiple_of(i * tq, tq), tq)

    def one_head(hh):
        cols = slice(hh * 2 * d, (hh + 1) * 2 * d)
        vx_ref[hh, :, :2 * d] = v_ref[:, cols]
        vx_ref[hh, :, 2 * d:] = jnp.ones((seq, 2 * d), BF16)

        def scores(i, slot):
            q = q_ref[rows(i), cols]
            qq = jnp.concatenate([jnp.where(in_map0, q, zero), jnp.where(in_map0, zero, q)], axis=0)
            s_refs[slot][...] = lax.dot_general(qq, k_ref[:, cols], (((1,), (1,)), ((), ())),
                                                preferred_element_type=F32)

        def combine(slot):
            s = s_refs[slot][...]
            e_refs[slot][...] = jnp.exp2(s - jnp.max(s, axis=-1, keepdims=True)).astype(BF16)

        def weighted_values(i, slot):
            ov = jnp.dot(e_refs[slot][...], vx_ref[hh], preferred_element_type=F32)
            r1 = 1.0 / ov[:tq, 2 * d:2 * d + 1]
            r2 = lam * (1.0 / ov[tq:, 2 * d:2 * d + 1])
            o = ov[:tq, :2 * d] * r1 - ov[tq:, :2 * d] * r2
            o = o * lax.rsqrt(jnp.mean(o * o, axis=-1, keepdims=True) + LN_EPS) * gain
            o_ref[rows(i), cols] = o.astype(o_ref.dtype)

        scores(0, 0)
        scores(1, 1)
        combine(0)

        def pair(j, carry):
            i = 2 * j + 1
            weighted_values(i - 1, 0)
            scores(i + 1, 0)
            combine(1)
            weighted_values(i, 1)
            scores(i + 2, 1)
            combine(0)
            return carry

        lax.fori_loop(0, nq // 2 - 1, pair, 0)
        weighted_values(nq - 2, 0)
        combine(1)
        weighted_values(nq - 1, 1)

    for hh in range(q_ref.shape[1] // (2 * d)):
        one_head(hh)


def _diff_attention(qkv, lam, subln_g, layer, lambda_init):
    b, seq, _ = qkv.shape
    hw = 2 * DIFF_HEAD_DIM
    assert hw == LANES and seq % (2 * DIFF_TQ) == 0
    perm = _head_block_perm(DIFF_HEAD_DIM)
    m1 = perm.index(DIFF_HEAD_DIM)
    assert all((p < DIFF_HEAD_DIM) == (l < m1 or 2 * m1 <= l < HALF_LANES + m1) for l, p in enumerate(perm))
    g = DIFF_HEADS_PER_STEP
    assert DIFF_HEADS % g == 0
    gw = g * hw
    k_off = DIFF_HEADS // g
    v_off = 2 * k_off
    kern = functools.partial(_diff_attn_kernel, tq=DIFF_TQ, lambda_init=lambda_init)
    return pl.pallas_call(
        kern,
        grid=(b, DIFF_HEADS // g),
        in_specs=[
            pl.BlockSpec((None, 4, DIFF_HEAD_DIM), lambda bi, h: (layer, 0, 0)),
            pl.BlockSpec((None, 1, hw), lambda bi, h: (layer, 0, 0)),
            pl.BlockSpec((None, seq, gw), lambda bi, h: (bi, 0, h)),
            pl.BlockSpec((None, seq, gw), lambda bi, h: (bi, 0, k_off + h)),
            pl.BlockSpec((None, seq, gw), lambda bi, h: (bi, 0, v_off + h)),
        ],
        out_specs=pl.BlockSpec((None, seq, gw), lambda bi, h: (bi, 0, h)),
        out_shape=jax.ShapeDtypeStruct((b, seq, D_MODEL), BF16),
        scratch_shapes=[pltpu.VMEM((2 * DIFF_TQ, seq), F32), pltpu.VMEM((2 * DIFF_TQ, seq), F32),
                        pltpu.VMEM((2 * DIFF_TQ, seq), BF16), pltpu.VMEM((2 * DIFF_TQ, seq), BF16),
                        pltpu.VMEM((g, seq, 2 * hw), BF16)],
        compiler_params=_params("parallel", "parallel"),
        name="diff_attn",
    )(lam, subln_g.reshape(-1, 1, hw), qkv, qkv, qkv)


def _win_attn_kernel(sink_ref, q_ref, kp_ref, kc_ref, kn_ref, vp_ref, vc_ref, vn_ref, o_ref, *, layer, seq):
    w, hd, rep = WINDOW, WIN_HEAD_DIM, WIN_REP
    grp = pl.program_id(1)
    n = pl.program_id(2)
    nsub = q_ref.shape[0] // w
    kband = jnp.concatenate([kp_ref[...], kc_ref[...], kn_ref[...]], axis=0)
    vband = jnp.concatenate([vp_ref[...], vc_ref[...], vn_ref[...]], axis=0)
    vband = jnp.concatenate([vband, jnp.ones(vband.shape, BF16)], axis=1)
    sink = jnp.concatenate(
        [jnp.full((w, hd), sink_ref[layer, grp * rep + r] * LOG2E, F32) for r in range(rep)], axis=0)
    qrow = lax.broadcasted_iota(jnp.int32, (rep * w, 3 * w), 0) & (w - 1)
    kcol = lax.broadcasted_iota(jnp.int32, (rep * w, 3 * w), 1)
    rel = kcol - qrow
    band_bias = jnp.where((rel >= 0) & (rel <= 2 * w), 0.0, -jnp.inf).astype(F32)
    kcol_row = lax.broadcasted_iota(jnp.int32, (1, 3 * w), 1)
    for sb in range(nsub):
        q4 = jnp.concatenate(
            [q_ref[sb * w:(sb + 1) * w, r * hd:(r + 1) * hd] for r in range(rep)], axis=0)
        kb = kband[sb * w:(sb + 3) * w]
        vb = vband[sb * w:(sb + 3) * w]
        s = lax.dot_general(q4, kb, (((1,), (1,)), ((), ())), preferred_element_type=F32)
        bias = band_bias
        if sb == 0 or sb == nsub - 1:
            kpos = (n * nsub + (sb - 1)) * w + kcol_row
            bias = bias + jnp.where((kpos >= 0) & (kpos < seq), 0.0, -jnp.inf)
        s = s + bias
        m = jnp.maximum(jnp.broadcast_to(jnp.max(s, axis=-1, keepdims=True), sink.shape), sink)
        e = jnp.concatenate([jnp.exp2(s[:, t * hd:(t + 1) * hd] - m) for t in range(3 * w // hd)], axis=1)
        ov = jnp.dot(e.astype(BF16), vb, preferred_element_type=F32)
        o4 = ov[:, :hd] * (1.0 / (ov[:, hd:] + jnp.exp2(sink - m)))
        for r in range(rep):
            o_ref[sb * w:(sb + 1) * w, r * hd:(r + 1) * hd] = o4[r * w:(r + 1) * w].astype(o_ref.dtype)


def _win_attention(qkv, sink, layer):
    b, seq, _ = qkv.shape
    w, hd = WINDOW, WIN_HEAD_DIM
    assert hd == LANES and seq % WIN_TQ == 0
    nsub = WIN_TQ // w
    nblk = seq // w
    gw = WIN_REP * hd
    k_off = WIN_Q_HEADS
    v_off = WIN_Q_HEADS + WIN_KV_HEADS

    def halo_prev(off):
        return pl.BlockSpec((None, w, hd), lambda bi, g, n: (bi, jnp.maximum(n * nsub - 1, 0), off + g))

    def halo_next(off):
        return pl.BlockSpec((None, w, hd), lambda bi, g, n: (bi, jnp.minimum((n + 1) * nsub, nblk - 1), off + g))

    def centre(off):
        return pl.BlockSpec((None, WIN_TQ, hd), lambda bi, g, n: (bi, n, off + g))

    return pl.pallas_call(
        functools.partial(_win_attn_kernel, layer=layer, seq=seq),
        grid=(b, WIN_KV_HEADS, seq // WIN_TQ),
        in_specs=[
            pl.BlockSpec(memory_space=pltpu.SMEM),
            pl.BlockSpec((None, WIN_TQ, gw), lambda bi, g, n: (bi, n, g)),
            halo_prev(k_off), centre(k_off), halo_next(k_off),
            halo_prev(v_off), centre(v_off), halo_next(v_off),
        ],
        out_specs=pl.BlockSpec((None, WIN_TQ, gw), lambda bi, g, n: (bi, n, g)),
        out_shape=jax.ShapeDtypeStruct((b, seq, D_MODEL), BF16),
        compiler_params=_params("parallel", "parallel", "parallel"),
        name="win_attn",
    )(sink, qkv, qkv, qkv, qkv, qkv, qkv, qkv)


def _proj_ln_kernel(a_ref, w_ref, x_ref, g_ref, b_ref, o_ref):
    hm = PROJ_SLAB
    for r0 in range(0, a_ref.shape[0], hm):
        y = jnp.dot(a_ref[r0:r0 + hm, :], w_ref[...], preferred_element_type=F32)
        o_ref[r0:r0 + hm, :] = _layer_norm_rows(ALPHA * x_ref[r0:r0 + hm, :] + y, g_ref[...], b_ref[...])


def _proj_res_ln(a2d, w, w_layer, x2d, g, b, layer):
    n, d = x2d.shape
    tm = PROJ_TM
    assert n % tm == 0
    row = pl.BlockSpec((tm, d), lambda i: (i, 0))
    vec = pl.BlockSpec((None, 1, d), lambda i: (layer, 0, 0))
    return pl.pallas_call(
        _proj_ln_kernel,
        grid=(n // tm,),
        in_specs=[row, pl.BlockSpec((None, d, d), lambda i: (w_layer, 0, 0)), row, vec, vec],
        out_specs=row,
        out_shape=jax.ShapeDtypeStruct((n, d), F32),
        compiler_params=_params("parallel"),
        name="proj_res_ln",
    )(a2d, w, x2d, g, b)


def _ffn_kernel(xp_ref, x_ref, xn_ref, wu_ref, cw_ref, cb_ref, wd_ref, g_ref, b_ref, o_ref, xs_ref,
                h_ref, act0_ref, act1_ref, *, tiles_per_seq):
    i = pl.program_id(0)
    c = pl.program_id(1)
    last = pl.num_programs(1) - 1
    tm = x_ref.shape[0]
    hl = FFN_HALO
    act_refs = (act0_ref, act1_ref)

    def up_project():
        rows = xs_ref.shape[0]
        split = (rows // 2 + BF16_ROWS - 1) // BF16_ROWS * BF16_ROWS
        for r0, r1 in ((0, split), (split, rows)):
            h = jnp.dot(xs_ref[r0:r1, :], wu_ref[...], preferred_element_type=F32)
            for j in range(h.shape[1] // LANES):
                h_ref[j, r0:r1, :] = h[:, j * LANES:(j + 1) * LANES]

    def conv_block(j):
        cols = slice(j * LANES, (j + 1) * LANES)
        return (h_ref[j, hl - 1:hl - 1 + tm, :] * cw_ref[0:1, cols] + h_ref[j, hl:hl + tm, :] * cw_ref[1:2, cols]
                + h_ref[j, hl + 1:hl + 1 + tm, :] * cw_ref[2:3, cols] + cb_ref[:, cols])

    def gate_into(act_ref):
        nk = act_ref.shape[1] // LANES
        for k in range(nk):
            gate = conv_block(k)
            up = conv_block(nk + k)
            act_ref[:, k * LANES:(k + 1) * LANES] = (gate * (1.0 / (1.0 + jnp.exp(-gate))) * up).astype(BF16)

    def down(slot):
        return jnp.dot(act_refs[slot][...], wd_ref[...], preferred_element_type=F32)

    @pl.when(c == 0)
    def _():
        t = i % tiles_per_seq
        xp = jnp.where(t == 0, 0.0, xp_ref[...])
        xn = jnp.where(t == tiles_per_seq - 1, 0.0, xn_ref[...])
        xs_ref[...] = jnp.concatenate([xp, x_ref[...], xn], axis=0).astype(BF16)
        o_ref[...] = ALPHA * x_ref[...]
        up_project()
        gate_into(act_refs[0])

    for parity in (0, 1):
        @pl.when((c > 0) & (c < last) & (lax.rem(c, 2) == parity))
        def _():
            up_project()
            o_ref[...] += down(1 - parity)
            gate_into(act_refs[parity])

    @pl.when(c == last)
    def _():
        z = o_ref[...] + down((FFN_NC - 1) % 2)
        o_ref[...] = _layer_norm_rows(z, g_ref[...], b_ref[...])


def _ffn_res_ln(x2d, seq, w_up, conv_w, conv_b, w_down, g, b, layer):
    n, d = x2d.shape
    tm, fc, hl = FFN_TM, FFN_FC, FFN_HALO
    assert n % tm == 0 and seq % tm == 0 and tm % hl == 0 and fc % MXU_COLS == 0
    hb = tm // hl
    kern = functools.partial(_ffn_kernel, tiles_per_seq=seq // tm)
    vec = pl.BlockSpec((None, 1, d), lambda i, c: (layer, 0, 0))
    nc = FFN_NC

    def up_chunk(i, c):
        return (layer, jnp.minimum(c, nc - 1), 0, 0)

    return pl.pallas_call(
        kern,
        grid=(n // tm, nc + 1),
        in_specs=[
            pl.BlockSpec((hl, d), lambda i, c: (jnp.maximum(i * hb - 1, 0), 0)),
            pl.BlockSpec((tm, d), lambda i, c: (i, 0), pipeline_mode=pl.Buffered(1)),
            pl.BlockSpec((hl, d), lambda i, c: (jnp.minimum((i + 1) * hb, n // hl - 1), 0)),
            pl.BlockSpec((None, None, d, 2 * fc), up_chunk),
            pl.BlockSpec((None, None, 3, 2 * fc), up_chunk),
            pl.BlockSpec((None, None, 1, 2 * fc), up_chunk),
            pl.BlockSpec((None, fc, d), lambda i, c: (layer, jnp.maximum(c - 1, 0), 0)),
            vec, vec,
        ],
        out_specs=pl.BlockSpec((tm, d), lambda i, c: (i, 0)),
        out_shape=jax.ShapeDtypeStruct((n, d), F32),
        scratch_shapes=[pltpu.VMEM((tm + 2 * hl, d), BF16), pltpu.VMEM((2 * fc // LANES, tm + 2 * hl, LANES), F32),
                        pltpu.VMEM((tm, fc), BF16), pltpu.VMEM((tm, fc), BF16)],
        compiler_params=_params("parallel", "arbitrary"),
        name="ffn_res_ln",
    )(x2d, x2d, x2d, w_up, conv_w, conv_b, w_down, g, b)


def _chunk_gate_up(t):
    fc = FFN_FC
    chunks = []
    for c0 in range(0, FFN_PAD, fc):
        valid = min(fc, FFN_DIM - c0)
        pieces = []
        for base in (0, FFN_DIM):
            pieces.append(t[..., base + c0:base + c0 + valid])
            if valid < fc:
                pieces.append(jnp.zeros(t.shape[:-1] + (fc - valid,), t.dtype))
        chunks.append(jnp.concatenate(pieces, axis=-1))
    return jnp.stack(chunks, axis=1)


def _tile_major(w, tn):
    l, d, c = w.shape
    return w.reshape(l, d, c // tn, tn).transpose(0, 2, 1, 3)


def _prep_weights(diff_w_in, diff_w_out, win_w_in, win_w_out, ffn_w_up, ffn_conv_w, ffn_conv_b, ffn_w_down):
    diff_qk_blocks = 4 * DIFF_HEADS * DIFF_HEAD_DIM // LANES
    win_qk_blocks = WIN_Q_HEADS + WIN_KV_HEADS
    return dict(
        diff_w_in=_tile_major(_permute_head_blocks(diff_w_in.astype(BF16), diff_qk_blocks, DIFF_HEAD_DIM),
                              DIFF_QKV_TN),
        diff_w_out=diff_w_out.astype(BF16),
        win_w_in=_tile_major(_permute_head_blocks(win_w_in.astype(BF16), win_qk_blocks, WIN_HEAD_DIM), WIN_QKV_TN),
        win_w_out=win_w_out.astype(BF16),
        up=_chunk_gate_up(ffn_w_up.astype(BF16)),
        cw=_chunk_gate_up(ffn_conv_w),
        cb=_chunk_gate_up(ffn_conv_b[:, None, :]),
        down=jnp.pad(ffn_w_down.astype(BF16), ((0, 0), (0, FFN_PAD - FFN_DIM), (0, 0))))


def _trunk(x, wts, diff_lam, diff_subln_g, win_sink, ln_mix_g, ln_mix_b, ln_ffn_g, ln_ffn_b):
    b, seq, d = x.shape
    n = b * seq
    x2d = x.reshape(n, d)
    diff_tabs = _rope_tables(seq, DIFF_HEAD_DIM, DIFF_HEAD_DIM ** -0.5 * LOG2E)
    win_tabs = _rope_tables(seq, WIN_HEAD_DIM, WIN_HEAD_DIM ** -0.5 * LOG2E)
    mix_g, mix_b = ln_mix_g[:, None, :], ln_mix_b[:, None, :]
    ffn_g, ffn_b = ln_ffn_g[:, None, :], ln_ffn_b[:, None, :]
    for i in range(DEPTH):
        j = i // N_MIXERS
        if i % N_MIXERS == 0:
            lambda_init = 0.8 - 0.6 * math.exp(-0.3 * i)
            qk = 2 * DIFF_HEADS * DIFF_HEAD_DIM
            qkv = _qkv_rope(x2d, wts["diff_w_in"], j, diff_tabs, seq, q_cols=qk, rope_cols=2 * qk)
            att = _diff_attention(qkv.reshape(b, seq, DIFF_IN), diff_lam, diff_subln_g, j, lambda_init)
            w_out = wts["diff_w_out"]
        else:
            qc = WIN_Q_HEADS * WIN_HEAD_DIM
            qkv = _qkv_rope(x2d, wts["win_w_in"], j, win_tabs, seq, q_cols=qc,
                            rope_cols=qc + WIN_KV_HEADS * WIN_HEAD_DIM)
            att = _win_attention(qkv.reshape(b, seq, WIN_IN), win_sink, j)
            w_out = wts["win_w_out"]
        x2d = _proj_res_ln(att.reshape(n, d), w_out, j, x2d, mix_g, mix_b, i)
        x2d = _ffn_res_ln(x2d, seq, wts["up"], wts["cw"], wts["cb"], wts["down"], ffn_g, ffn_b, i)
    return x2d.reshape(b, seq, d)


def kernel(x_prompt, x_sample, diff_w_in, diff_lam, diff_subln_g, diff_w_out, win_w_in, win_sink, win_w_out,
           ln_mix_g, ln_mix_b, ffn_w_up, ffn_conv_w, ffn_conv_b, ffn_w_down, ln_ffn_g, ln_ffn_b):
    wts = _prep_weights(diff_w_in, diff_w_out, win_w_in, win_w_out, ffn_w_up, ffn_conv_w, ffn_conv_b, ffn_w_down)
    rest = (diff_lam, diff_subln_g, win_sink, ln_mix_g, ln_mix_b, ln_ffn_g, ln_ffn_b)
    return (_trunk(x_prompt, wts, *rest), _trunk(x_sample, wts, *rest))
```

```python
import functools
import math

import jax
import jax.numpy as jnp
from jax import lax
from jax.experimental import pallas as pl
from jax.experimental.pallas import tpu as pltpu

D_MODEL = 2048
DEPTH = 4
N_MIXERS = 2
DIFF_HEAD_DIM = 64
DIFF_HEADS = D_MODEL // (2 * DIFF_HEAD_DIM)
DIFF_IN = 6 * DIFF_HEADS * DIFF_HEAD_DIM
WIN_HEAD_DIM = 128
WIN_Q_HEADS = D_MODEL // WIN_HEAD_DIM
WIN_KV_HEADS = 4
WIN_REP = WIN_Q_HEADS // WIN_KV_HEADS
WIN_IN = (WIN_Q_HEADS + 2 * WIN_KV_HEADS) * WIN_HEAD_DIM
WINDOW = 128
FFN_DIM = ((8 * D_MODEL // 3 + 127) // 128) * 128
ROPE_THETA = 500000.0
ROPE_FRACTION = 4
ALPHA = (2 * DEPTH) ** 0.25
LN_EPS = 1e-5
LOG2E = math.log2(math.e)

LANES = 128
HALF_LANES = LANES // 2
BF16_ROWS = 16
MXU_COLS = 256
VMEM_LIMIT = 56 * 1024 * 1024

QKV_TM = 1024
DIFF_QKV_TN = 2048
WIN_QKV_TN = 1024
PROJ_TM = 512
PROJ_SLAB = 128
FFN_TM = 1024
FFN_FC = 512
FFN_LN_SLAB = 256
FFN_NC = (FFN_DIM + FFN_FC - 1) // FFN_FC
FFN_PAD = FFN_NC * FFN_FC
F32_ROWS = 8
FFN_HALO = F32_ROWS
DIFF_TQ = 256
DIFF_HEADS_PER_STEP = 2
WIN_TQ = 1024

F32 = jnp.float32
BF16 = jnp.bfloat16


def _params(*sem):
    return pltpu.CompilerParams(dimension_semantics=sem, vmem_limit_bytes=VMEM_LIMIT)


def _layer_norm_rows(z, g, b):
    mu = jnp.mean(z, axis=-1, keepdims=True)
    d = z - mu
    var = jnp.mean(d * d, axis=-1, keepdims=True)
    return d * lax.rsqrt(var + LN_EPS) * g + b


def _head_block_perm(head_dim):
    rot = head_dim // ROPE_FRACTION
    half = rot // 2
    x1, x2, rest = [], [], []
    for m in range(LANES // head_dim):
        base = m * head_dim
        x1 += range(base, base + half)
        x2 += range(base + half, base + rot)
        rest += range(base + rot, base + head_dim)
    n_low = HALF_LANES - len(x1)
    return x1 + rest[:n_low] + x2 + rest[n_low:]


def _permute_head_blocks(w, n_blocks, head_dim):
    perm = _head_block_perm(head_dim)
    runs, start = [], 0
    for i in range(1, LANES + 1):
        if i == LANES or perm[i] != perm[i - 1] + 1:
            runs.append((perm[start], perm[i - 1] + 1))
            start = i
    lead = w.shape[:-1]
    blk = w[..., :n_blocks * LANES].reshape(*lead, n_blocks, LANES)
    blk = jnp.concatenate([blk[..., a:b] for a, b in runs], axis=-1)
    return jnp.concatenate([blk.reshape(*lead, n_blocks * LANES), w[..., n_blocks * LANES:]], axis=-1)


def _rope_tables(seq, head_dim, q_scale):
    rot = head_dim // ROPE_FRACTION
    reps = LANES // head_dim
    pos = jnp.arange(seq, dtype=F32)
    inv_freq = ROPE_THETA ** (-jnp.arange(0, rot, 2, dtype=F32) / rot)
    ang = pos[:, None] * inv_freq[None, :]
    cos, sin = jnp.tile(jnp.cos(ang), (1, reps)), jnp.tile(jnp.sin(ang), (1, reps))
    pad = HALF_LANES - cos.shape[1]
    ones, zeros = jnp.ones((seq, pad), F32), jnp.zeros((seq, pad), F32)
    c = jnp.concatenate([cos, ones, cos, ones], axis=-1)
    s = jnp.concatenate([-sin, zeros, sin, zeros], axis=-1)
    k_tab = jnp.stack([c, s])
    return jnp.stack([k_tab * q_scale, k_tab])


def _qkv_kernel(x_ref, w_ref, tab_ref, o_ref, xs_ref, *, n_full_tiles, tail_rope_blocks):
    j = pl.program_id(1)

    @pl.when(j == 0)
    def _():
        xs_ref[...] = x_ref[...].astype(BF16)

    def tile(rope_blocks):
        hm = x_ref.shape[0] // 2
        for r0 in (0, hm):
            acc = jnp.dot(xs_ref[r0:r0 + hm, :], w_ref[...], preferred_element_type=F32)
            c, s = tab_ref[0, r0:r0 + hm, :], tab_ref[1, r0:r0 + hm, :]
            for t in range(acc.shape[1] // LANES):
                y = acc[:, t * LANES:(t + 1) * LANES]
                if t < rope_blocks:
                    y = y * c + pltpu.roll(y, HALF_LANES, 1) * s
                o_ref[r0:r0 + hm, t * LANES:(t + 1) * LANES] = y.astype(o_ref.dtype)

    @pl.when(j < n_full_tiles)
    def _():
        tile(o_ref.shape[1] // LANES)

    @pl.when(j >= n_full_tiles)
    def _():
        tile(tail_rope_blocks)


def _qkv_rope(x2d, w, layer, tabs, seq, *, q_cols, rope_cols):
    n, d = x2d.shape
    tm, tn = QKV_TM, w.shape[3]
    assert w.shape[2] == d
    c = w.shape[1] * tn
    assert n % tm == 0 and seq % tm == 0 and q_cols % tn == 0 and rope_cols % LANES == 0
    n_q_tiles = q_cols // tn
    n_full_tiles = rope_cols // tn
    assert c // tn - n_full_tiles <= 1 or rope_cols % tn == 0
    seq_tiles = seq // tm
    kern = functools.partial(_qkv_kernel, n_full_tiles=n_full_tiles, tail_rope_blocks=rope_cols % tn // LANES)

    def tab_index(i, j):
        return ((j >= n_q_tiles).astype(jnp.int32), 0, i % seq_tiles, 0)

    return pl.pallas_call(
        kern,
        grid=(n // tm, c // tn),
        in_specs=[
            pl.BlockSpec((tm, d), lambda i, j: (i, 0)),
            pl.BlockSpec((None, None, d, tn), lambda i, j: (layer, j, 0, 0)),
            pl.BlockSpec((None, 2, tm, LANES), tab_index),
        ],
        out_specs=pl.BlockSpec((tm, tn), lambda i, j: (i, j)),
        out_shape=jax.ShapeDtypeStruct((n, c), BF16),
        scratch_shapes=[pltpu.VMEM((tm, d), BF16)],
        compiler_params=_params("parallel", "arbitrary"),
        name="qkv_rope",
    )(x2d, w, tabs)


def _diff_attn_kernel(lam_ref, g_ref, q_ref, k_ref, v_ref, o_ref, s0_ref, s1_ref, e0_ref, e1_ref, vx_ref, *,
                      tq, lambda_init):
    seq = q_ref.shape[0]
    nq = seq // tq
    d = DIFF_HEAD_DIM
    lf = lam_ref[...]
    lam = (jnp.exp(jnp.sum(lf[0:1] * lf[1:2], axis=-1, keepdims=True))
           - jnp.exp(jnp.sum(lf[2:3] * lf[3:4], axis=-1, keepdims=True)) + lambda_init)
    gain = g_ref[...] * (1.0 - lambda_init)
    lane = lax.broadcasted_iota(jnp.int32, (tq, 2 * d), 1)
    perm = _head_block_perm(d)
    map1_start_low = perm.index(d)
    map1_start_high = HALF_LANES + map1_start_low
    in_map0 = (lane < map1_start_low) | ((lane >= 2 * map1_start_low) & (lane < map1_start_high))
    zero = jnp.zeros((tq, 2 * d), BF16)
    s_refs, e_refs = (s0_ref, s1_ref), (e0_ref, e1_ref)

    def rows(i):
        return pl.ds(pl.multiple_of(i * tq, tq), tq)

    def one_head(hh):
        cols = slice(hh * 2 * d, (hh + 1) * 2 * d)
        vx_ref[hh, :, :2 * d] = v_ref[:, cols]
        vx_ref[hh, :, 2 * d:] = jnp.ones((seq, 2 * d), BF16)

        def scores(i, slot):
            q = q_ref[rows(i), cols]
            qq = jnp.concatenate([jnp.where(in_map0, q, zero), jnp.where(in_map0, zero, q)], axis=0)
            s_refs[slot][...] = lax.dot_general(qq, k_ref[:, cols], (((1,), (1,)), ((), ())),
                                                preferred_element_type=F32)

        def combine(slot):
            s = s_refs[slot][...]
            e_refs[slot][...] = jnp.exp2(s - jnp.max(s, axis=-1, keepdims=True)).astype(BF16)

        def weighted_values(i, slot):
            ov = jnp.dot(e_refs[slot][...], vx_ref[hh], preferred_element_type=F32)
            r1 = 1.0 / ov[:tq, 2 * d:2 * d + 1]
            r2 = lam * (1.0 / ov[tq:, 2 * d:2 * d + 1])
            o = ov[:tq, :2 * d] * r1 - ov[tq:, :2 * d] * r2
            o = o * lax.rsqrt(jnp.mean(o * o, axis=-1, keepdims=True) + LN_EPS) * gain
            o_ref[rows(i), cols] = o.astype(o_ref.dtype)

        scores(0, 0)
        scores(1, 1)
        combine(0)

        def pair(j, carry):
            i = 2 * j + 1
            weighted_values(i - 1, 0)
            scores(i + 1, 0)
            combine(1)
            weighted_values(i, 1)
            scores(i + 2, 1)
            combine(0)
            return carry

        lax.fori_loop(0, nq // 2 - 1, pair, 0)
        weighted_values(nq - 2, 0)
        combine(1)
        weighted_values(nq - 1, 1)

    for hh in range(q_ref.shape[1] // (2 * d)):
        one_head(hh)


def _diff_attention(qkv, lam, subln_g, layer, lambda_init):
    b, seq, _ = qkv.shape
    hw = 2 * DIFF_HEAD_DIM
    assert hw == LANES and seq % (2 * DIFF_TQ) == 0
    perm = _head_block_perm(DIFF_HEAD_DIM)
    m1 = perm.index(DIFF_HEAD_DIM)
    assert all((p < DIFF_HEAD_DIM) == (l < m1 or 2 * m1 <= l < HALF_LANES + m1) for l, p in enumerate(perm))
    g = DIFF_HEADS_PER_STEP
    assert DIFF_HEADS % g == 0
    gw = g * hw
    k_off = DIFF_HEADS // g
    v_off = 2 * k_off
    kern = functools.partial(_diff_attn_kernel, tq=DIFF_TQ, lambda_init=lambda_init)
    return pl.pallas_call(
        kern,
        grid=(b, DIFF_HEADS // g),
        in_specs=[
            pl.BlockSpec((None, 4, DIFF_HEAD_DIM), lambda bi, h: (layer, 0, 0)),
            pl.BlockSpec((None, 1, hw), lambda bi, h: (layer, 0, 0)),
            pl.BlockSpec((None, seq, gw), lambda bi, h: (bi, 0, h)),
            pl.BlockSpec((None, seq, gw), lambda bi, h: (bi, 0, k_off + h)),
            pl.BlockSpec((None, seq, gw), lambda bi, h: (bi, 0, v_off + h)),
        ],
        out_specs=pl.BlockSpec((None, seq, gw), lambda bi, h: (bi, 0, h)),
        out_shape=jax.ShapeDtypeStruct((b, seq, D_MODEL), BF16),
        scratch_shapes=[pltpu.VMEM((2 * DIFF_TQ, seq), F32), pltpu.VMEM((2 * DIFF_TQ, seq), F32),
                        pltpu.VMEM((2 * DIFF_TQ, seq), BF16), pltpu.VMEM((2 * DIFF_TQ, seq), BF16),
                        pltpu.VMEM((g, seq, 2 * hw), BF16)],
        compiler_params=_params("parallel", "parallel"),
        name="diff_attn",
    )(lam, subln_g.reshape(-1, 1, hw), qkv, qkv, qkv)


def _win_attn_kernel(sink_ref, q_ref, kp_ref, kc_ref, kn_ref, vp_ref, vc_ref, vn_ref, o_ref, *, layer, seq):
    w, hd, rep = WINDOW, WIN_HEAD_DIM, WIN_REP
    grp = pl.program_id(1)
    n = pl.program_id(2)
    nsub = q_ref.shape[0] // w
    kband = jnp.concatenate([kp_ref[...], kc_ref[...], kn_ref[...]], axis=0)
    vband = jnp.concatenate([vp_ref[...], vc_ref[...], vn_ref[...]], axis=0)
    vband = jnp.concatenate([vband, jnp.ones(vband.shape, BF16)], axis=1)
    sink = jnp.concatenate(
        [jnp.full((w, hd), sink_ref[layer, grp * rep + r] * LOG2E, F32) for r in range(rep)], axis=0)
    qrow = lax.broadcasted_iota(jnp.int32, (rep * w, 3 * w), 0) & (w - 1)
    kcol = lax.broadcasted_iota(jnp.int32, (rep * w, 3 * w), 1)
    rel = kcol - qrow
    band_bias = jnp.where((rel >= 0) & (rel <= 2 * w), 0.0, -jnp.inf).astype(F32)
    kcol_row = lax.broadcasted_iota(jnp.int32, (1, 3 * w), 1)
    for sb in range(nsub):
        q4 = jnp.concatenate(
            [q_ref[sb * w:(sb + 1) * w, r * hd:(r + 1) * hd] for r in range(rep)], axis=0)
        kb = kband[sb * w:(sb + 3) * w]
        vb = vband[sb * w:(sb + 3) * w]
        s = lax.dot_general(q4, kb, (((1,), (1,)), ((), ())), preferred_element_type=F32)
        bias = band_bias
        if sb == 0 or sb == nsub - 1:
            kpos = (n * nsub + (sb - 1)) * w + kcol_row
            bias = bias + jnp.where((kpos >= 0) & (kpos < seq), 0.0, -jnp.inf)
        s = s + bias
        m = jnp.maximum(jnp.broadcast_to(jnp.max(s, axis=-1, keepdims=True), sink.shape), sink)
        e = jnp.concatenate([jnp.exp2(s[:, t * hd:(t + 1) * hd] - m) for t in range(3 * w // hd)], axis=1)
        ov = jnp.dot(e.astype(BF16), vb, preferred_element_type=F32)
        o4 = ov[:, :hd] * (1.0 / (ov[:, hd:] + jnp.exp2(sink - m)))
        for r in range(rep):
            o_ref[sb * w:(sb + 1) * w, r * hd:(r + 1) * hd] = o4[r * w:(r + 1) * w].astype(o_ref.dtype)


def _win_attention(qkv, sink, layer):
    b, seq, _ = qkv.shape
    w, hd = WINDOW, WIN_HEAD_DIM
    assert hd == LANES and seq % WIN_TQ == 0
    nsub = WIN_TQ // w
    nblk = seq // w
    gw = WIN_REP * hd
    k_off = WIN_Q_HEADS
    v_off = WIN_Q_HEADS + WIN_KV_HEADS

    def halo_prev(off):
        return pl.BlockSpec((None, w, hd), lambda bi, g, n: (bi, jnp.maximum(n * nsub - 1, 0), off + g))

    def halo_next(off):
        return pl.BlockSpec((None, w, hd), lambda bi, g, n: (bi, jnp.minimum((n + 1) * nsub, nblk - 1), off + g))

    def centre(off):
        return pl.BlockSpec((None, WIN_TQ, hd), lambda bi, g, n: (bi, n, off + g))

    return pl.pallas_call(
        functools.partial(_win_attn_kernel, layer=layer, seq=seq),
        grid=(b, WIN_KV_HEADS, seq // WIN_TQ),
        in_specs=[
            pl.BlockSpec(memory_space=pltpu.SMEM),
            pl.BlockSpec((None, WIN_TQ, gw), lambda bi, g, n: (bi, n, g)),
            halo_prev(k_off), centre(k_off), halo_next(k_off),
            halo_prev(v_off), centre(v_off), halo_next(v_off),
        ],
        out_specs=pl.BlockSpec((None, WIN_TQ, gw), lambda bi, g, n: (bi, n, g)),
        out_shape=jax.ShapeDtypeStruct((b, seq, D_MODEL), BF16),
        compiler_params=_params("parallel", "parallel", "parallel"),
        name="win_attn",
    )(sink, qkv, qkv, qkv, qkv, qkv, qkv, qkv)


def _proj_ln_kernel(a_ref, w_ref, x_ref, g_ref, b_ref, o_ref):
    hm = PROJ_SLAB
    for r0 in range(0, a_ref.shape[0], hm):
        y = jnp.dot(a_ref[r0:r0 + hm, :], w_ref[...], preferred_element_type=F32)
        o_ref[r0:r0 + hm, :] = _layer_norm_rows(ALPHA * x_ref[r0:r0 + hm, :] + y, g_ref[...], b_ref[...])


def _proj_res_ln(a2d, w, w_layer, x2d, g, b, layer):
    n, d = x2d.shape
    tm = PROJ_TM
    assert n % tm == 0
    row = pl.BlockSpec((tm, d), lambda i: (i, 0))
    vec = pl.BlockSpec((None, 1, d), lambda i: (layer, 0, 0))
    return pl.pallas_call(
        _proj_ln_kernel,
        grid=(n // tm,),
        in_specs=[row, pl.BlockSpec((None, d, d), lambda i: (w_layer, 0, 0)), row, vec, vec],
        out_specs=row,
        out_shape=jax.ShapeDtypeStruct((n, d), F32),
        compiler_params=_params("parallel"),
        name="proj_res_ln",
    )(a2d, w, x2d, g, b)


def _ffn_kernel(xp_ref, x_ref, xn_ref, wu_ref, cw_ref, cb_ref, wd_ref, g_ref, b_ref, o_ref, xs_ref,
                h_ref, act0_ref, act1_ref, *, tiles_per_seq):
    i = pl.program_id(0)
    c = pl.program_id(1)
    last = pl.num_programs(1) - 1
    tm = x_ref.shape[0]
    hl = FFN_HALO
    act_refs = (act0_ref, act1_ref)

    def up_project():
        rows = xs_ref.shape[0]
        split = (rows // 2 + BF16_ROWS - 1) // BF16_ROWS * BF16_ROWS
        for r0, r1 in ((0, split), (split, rows)):
            h = jnp.dot(xs_ref[r0:r1, :], wu_ref[...], preferred_element_type=F32)
            for j in range(h.shape[1] // LANES):
                h_ref[j, r0:r1, :] = h[:, j * LANES:(j + 1) * LANES]

    def conv_block(j, r0, nr):
        cols = slice(j * LANES, (j + 1) * LANES)
        return (h_ref[j, hl - 1 + r0:hl - 1 + r0 + nr, :] * cw_ref[0:1, cols]
                + h_ref[j, hl + r0:hl + r0 + nr, :] * cw_ref[1:2, cols]
                + h_ref[j, hl + 1 + r0:hl + 1 + r0 + nr, :] * cw_ref[2:3, cols] + cb_ref[:, cols])

    def gate_into(act_ref):
        nk = act_ref.shape[1] // LANES
        nr = tm // 2
        for r0 in (0, nr):
            for k in range(nk):
                gate = conv_block(k, r0, nr)
                up = conv_block(nk + k, r0, nr)
                act_ref[r0:r0 + nr, k * LANES:(k + 1) * LANES] = (
                    gate * (1.0 / (1.0 + jnp.exp(-gate))) * up).astype(BF16)

    def down(slot, r0=0, nr=None):
        nr = tm if nr is None else nr
        return jnp.dot(act_refs[slot][r0:r0 + nr, :], wd_ref[...], preferred_element_type=F32)

    @pl.when(c == 0)
    def _():
        t = i % tiles_per_seq
        xp = jnp.where(t == 0, 0.0, xp_ref[...])
        xn = jnp.where(t == tiles_per_seq - 1, 0.0, xn_ref[...])
        xs_ref[...] = jnp.concatenate([xp, x_ref[...], xn], axis=0).astype(BF16)
        o_ref[...] = ALPHA * x_ref[...]
        up_project()
        gate_into(act_refs[0])

    for parity in (0, 1):
        @pl.when((c > 0) & (c < last) & (lax.rem(c, 2) == parity))
        def _():
            up_project()
            o_ref[...] += down(1 - parity)
            gate_into(act_refs[parity])

    @pl.when(c == last)
    def _():
        nr = FFN_LN_SLAB
        for r0 in range(0, tm, nr):
            z = o_ref[r0:r0 + nr, :] + down((FFN_NC - 1) % 2, r0, nr)
            o_ref[r0:r0 + nr, :] = _layer_norm_rows(z, g_ref[...], b_ref[...])


def _ffn_res_ln(x2d, seq, w_up, conv_w, conv_b, w_down, g, b, layer):
    n, d = x2d.shape
    tm, fc, hl = FFN_TM, FFN_FC, FFN_HALO
    assert n % tm == 0 and seq % tm == 0 and tm % hl == 0 and fc % MXU_COLS == 0
    hb = tm // hl
    kern = functools.partial(_ffn_kernel, tiles_per_seq=seq // tm)
    vec = pl.BlockSpec((None, 1, d), lambda i, c: (layer, 0, 0))
    nc = FFN_NC

    def up_chunk(i, c):
        return (layer, jnp.minimum(c, nc - 1), 0, 0)

    return pl.pallas_call(
        kern,
        grid=(n // tm, nc + 1),
        in_specs=[
            pl.BlockSpec((hl, d), lambda i, c: (jnp.maximum(i * hb - 1, 0), 0)),
            pl.BlockSpec((tm, d), lambda i, c: (i, 0), pipeline_mode=pl.Buffered(1)),
            pl.BlockSpec((hl, d), lambda i, c: (jnp.minimum((i + 1) * hb, n // hl - 1), 0)),
            pl.BlockSpec((None, None, d, 2 * fc), up_chunk),
            pl.BlockSpec((None, None, 3, 2 * fc), up_chunk),
            pl.BlockSpec((None, None, 1, 2 * fc), up_chunk),
            pl.BlockSpec((None, fc, d), lambda i, c: (layer, jnp.maximum(c - 1, 0), 0)),
            vec, vec,
        ],
        out_specs=pl.BlockSpec((tm, d), lambda i, c: (i, 0)),
        out_shape=jax.ShapeDtypeStruct((n, d), F32),
        scratch_shapes=[pltpu.VMEM((tm + 2 * hl, d), BF16), pltpu.VMEM((2 * fc // LANES, tm + 2 * hl, LANES), F32),
                        pltpu.VMEM((tm, fc), BF16), pltpu.VMEM((tm, fc), BF16)],
        compiler_params=_params("parallel", "arbitrary"),
        name="ffn_res_ln",
    )(x2d, x2d, x2d, w_up, conv_w, conv_b, w_down, g, b)


def _chunk_gate_up(t):
    fc = FFN_FC
    chunks = []
    for c0 in range(0, FFN_PAD, fc):
        valid = min(fc, FFN_DIM - c0)
        pieces = []
        for base in (0, FFN_DIM):
            pieces.append(t[..., base + c0:base + c0 + valid])
            if valid < fc:
                pieces.append(jnp.zeros(t.shape[:-1] + (fc - valid,), t.dtype))
        chunks.append(jnp.concatenate(pieces, axis=-1))
    return jnp.stack(chunks, axis=1)


def _tile_major(w, tn):
    l, d, c = w.shape
    return w.reshape(l, d, c // tn, tn).transpose(0, 2, 1, 3)


def _prep_weights(diff_w_in, diff_w_out, win_w_in, win_w_out, ffn_w_up, ffn_conv_w, ffn_conv_b, ffn_w_down):
    diff_qk_blocks = 4 * DIFF_HEADS * DIFF_HEAD_DIM // LANES
    win_qk_blocks = WIN_Q_HEADS + WIN_KV_HEADS
    return dict(
        diff_w_in=_tile_major(_permute_head_blocks(diff_w_in.astype(BF16), diff_qk_blocks, DIFF_HEAD_DIM),
                              DIFF_QKV_TN),
        diff_w_out=diff_w_out.astype(BF16),
        win_w_in=_tile_major(_permute_head_blocks(win_w_in.astype(BF16), win_qk_blocks, WIN_HEAD_DIM), WIN_QKV_TN),
        win_w_out=win_w_out.astype(BF16),
        up=_chunk_gate_up(ffn_w_up.astype(BF16)),
        cw=_chunk_gate_up(ffn_conv_w),
        cb=_chunk_gate_up(ffn_conv_b[:, None, :]),
        down=jnp.pad(ffn_w_down.astype(BF16), ((0, 0), (0, FFN_PAD - FFN_DIM), (0, 0))))


def _trunk(x, wts, diff_lam, diff_subln_g, win_sink, ln_mix_g, ln_mix_b, ln_ffn_g, ln_ffn_b):
    b, seq, d = x.shape
    n = b * seq
    x2d = x.reshape(n, d)
    diff_tabs = _rope_tables(seq, DIFF_HEAD_DIM, DIFF_HEAD_DIM ** -0.5 * LOG2E)
    win_tabs = _rope_tables(seq, WIN_HEAD_DIM, WIN_HEAD_DIM ** -0.5 * LOG2E)
    mix_g, mix_b = ln_mix_g[:, None, :], ln_mix_b[:, None, :]
    ffn_g, ffn_b = ln_ffn_g[:, None, :], ln_ffn_b[:, None, :]
    for i in range(DEPTH):
        j = i // N_MIXERS
        if i % N_MIXERS == 0:
            lambda_init = 0.8 - 0.6 * math.exp(-0.3 * i)
            qk = 2 * DIFF_HEADS * DIFF_HEAD_DIM
            qkv = _qkv_rope(x2d, wts["diff_w_in"], j, diff_tabs, seq, q_cols=qk, rope_cols=2 * qk)
            att = _diff_attention(qkv.reshape(b, seq, DIFF_IN), diff_lam, diff_subln_g, j, lambda_init)
            w_out = wts["diff_w_out"]
        else:
            qc = WIN_Q_HEADS * WIN_HEAD_DIM
            qkv = _qkv_rope(x2d, wts["win_w_in"], j, win_tabs, seq, q_cols=qc,
                            rope_cols=qc + WIN_KV_HEADS * WIN_HEAD_DIM)
            att = _win_attention(qkv.reshape(b, seq, WIN_IN), win_sink, j)
            w_out = wts["win_w_out"]
        x2d = _proj_res_ln(att.reshape(n, d), w_out, j, x2d, mix_g, mix_b, i)
        x2d = _ffn_res_ln(x2d, seq, wts["up"], wts["cw"], wts["cb"], wts["down"], ffn_g, ffn_b, i)
    return x2d.reshape(b, seq, d)


def kernel(x_prompt, x_sample, diff_w_in, diff_lam, diff_subln_g, diff_w_out, win_w_in, win_sink, win_w_out,
           ln_mix_g, ln_mix_b, ffn_w_up, ffn_conv_w, ffn_conv_b, ffn_w_down, ln_ffn_g, ln_ffn_b):
    wts = _prep_weights(diff_w_in, diff_w_out, win_w_in, win_w_out, ffn_w_up, ffn_conv_w, ffn_conv_b, ffn_w_down)
    rest = (diff_lam, diff_subln_g, win_sink, ln_mix_g, ln_mix_b, ln_ffn_g, ln_ffn_b)
    return (_trunk(x_prompt, wts, *rest), _trunk(x_sample, wts, *rest))
```
